```python
import jax, jax.numpy as jnp
from jax import lax
import numpy as np

D_MODEL = 1024
BATCH = 8
SEQ = 2048
DEPTH = 4
DEC_BATCH = 128
DEC_SEQ = 8
PAST_LEN = 16384
PAGE_SIZE = 128

N_MEM = 256
EXPAND = 2
BRANCH_WIDTH = EXPAND * D_MODEL
XA_HEADS = 4
XA_HEAD_DIM = D_MODEL // 8
XA_WIDTH = XA_HEADS * XA_HEAD_DIM
MIX_WIDTH = BRANCH_WIDTH - XA_WIDTH
SHORT_CONV_W = 3
CONFORMER_CONV_W = 31
N_A = (DEPTH + 1) // 2
N_B = DEPTH // 2
A_IN_WIDTH = 3 * MIX_WIDTH + BRANCH_WIDTH + XA_WIDTH
B_IN_WIDTH = 2 * MIX_WIDTH + BRANCH_WIDTH + XA_WIDTH
EPS = 1e-6

kernel_name = "hybrid_conv_memory_decoder_step"


def rms_norm(x, g):
    xf = x.astype(jnp.float32)
    y = xf * lax.rsqrt(jnp.mean(xf * xf, axis=-1, keepdims=True) + EPS)
    return (y * g.astype(jnp.float32)).astype(x.dtype)


def layer_norm(x, g, b):
    xf = x.astype(jnp.float32)
    mu = jnp.mean(xf, axis=-1, keepdims=True)
    xc = xf - mu
    var = jnp.mean(xc * xc, axis=-1, keepdims=True)
    y = xc * lax.rsqrt(var + EPS) * g.astype(jnp.float32) + b.astype(jnp.float32)
    return y.astype(x.dtype)


def causal_dwconv(u, state, w):
    full = jnp.concatenate([state.astype(u.dtype), u], axis=1)
    y = lax.conv_general_dilated(
        full, w[:, None, :].astype(u.dtype), window_strides=(1,), padding="VALID",
        dimension_numbers=("NWC", "WIO", "NWC"), feature_group_count=u.shape[-1])
    return y, full[:, -(w.shape[0] - 1):, :]


def memory_kv(mem, g, w_kv):
    b, m, _ = mem.shape
    kv = jnp.einsum("bmd,de->bme", rms_norm(mem, g), w_kv)
    k, v = jnp.split(kv, 2, axis=-1)
    return (k.reshape(b, m, XA_HEADS, XA_HEAD_DIM), v.reshape(b, m, XA_HEADS, XA_HEAD_DIM))


def memory_attention(q, k, v):
    b, t, _ = q.shape
    qh = q.reshape(b, t, XA_HEADS, XA_HEAD_DIM)
    s = jnp.einsum("bthd,bmhd->bhtm", qh, k).astype(jnp.float32) * (XA_HEAD_DIM ** -0.5)
    p = jax.nn.softmax(s, axis=-1).astype(v.dtype)
    o = jnp.einsum("bhtm,bmhd->bthd", p, v)
    return o.reshape(b, t, XA_WIDTH)


def short_conv_layer(x, conv_state, mem_k, mem_v, g_pre, g_post, w_in, conv_w, w_out):
    proj = jnp.einsum("btd,de->bte", rms_norm(x, g_pre), w_in)
    hin, b_gate, c_gate, z, q = jnp.split(
        proj, [MIX_WIDTH, 2 * MIX_WIDTH, 3 * MIX_WIDTH, 3 * MIX_WIDTH + BRANCH_WIDTH], axis=-1)
    conv, new_state = causal_dwconv(c_gate * hin, conv_state, conv_w)
    mix = b_gate * conv
    xa = memory_attention(q, mem_k, mem_v)
    branch = jnp.concatenate([mix, xa], axis=-1) * jax.nn.silu(z)
    out = jnp.einsum("bte,ed->btd", branch, w_out)
    return x + rms_norm(out, g_post), new_state


def conformer_layer(x, conv_state, mem_k, mem_v, g_pre, g_post, w_in, conv_w, conv_b,
                    ln_g, ln_b, w_out):
    proj = jnp.einsum("btd,de->bte", rms_norm(x, g_pre), w_in)
    val, glu, z, q = jnp.split(
        proj, [MIX_WIDTH, 2 * MIX_WIDTH, 2 * MIX_WIDTH + BRANCH_WIDTH], axis=-1)
    u = val * jax.nn.sigmoid(glu)
    conv, new_state = causal_dwconv(u, conv_state, conv_w)
    mix = jax.nn.silu(layer_norm(conv + conv_b, ln_g, ln_b))
    xa = memory_attention(q, mem_k, mem_v)
    branch = jnp.concatenate([mix, xa], axis=-1) * jax.nn.silu(z)
    out = jnp.einsum("bte,ed->btd", branch, w_out)
    return x + rms_norm(out, g_post), new_state


def setup_inputs(seed: int = 0) -> dict:
    key = jax.random.key(seed)
    ks = iter(jax.random.split(key, 40))
    f32 = jnp.float32

    def nrm(shape, scale=1.0):
        return jax.random.normal(next(ks), shape, f32) * scale

    def gain(shape):
        return 1.0 + nrm(shape, 0.05)

    return {
        "x_prompt": nrm((BATCH, SEQ, D_MODEL)),
        "x_sample": nrm((DEC_BATCH, DEC_SEQ, D_MODEL)),
        "mem_prompt": nrm((BATCH, N_MEM, D_MODEL)),
        "cache_mem_k": nrm((DEPTH, DEC_BATCH, N_MEM, XA_HEADS, XA_HEAD_DIM)),
        "cache_mem_v": nrm((DEPTH, DEC_BATCH, N_MEM, XA_HEADS, XA_HEAD_DIM)),
        "state_conv_a": nrm((N_A, DEC_BATCH, SHORT_CONV_W - 1, MIX_WIDTH)),
        "state_conv_b": nrm((N_B, DEC_BATCH, CONFORMER_CONV_W - 1, MIX_WIDTH)),
        "a_norm_pre": gain((N_A, D_MODEL)),
        "a_norm_post": gain((N_A, D_MODEL)),
        "a_mem_norm": gain((N_A, D_MODEL)),
        "a_w_in": nrm((N_A, D_MODEL, A_IN_WIDTH), D_MODEL ** -0.5),
        "a_conv_w": nrm((N_A, SHORT_CONV_W, MIX_WIDTH), SHORT_CONV_W ** -0.5),
        "a_w_kv": nrm((N_A, D_MODEL, 2 * XA_WIDTH), D_MODEL ** -0.5),
        "a_w_out": nrm((N_A, BRANCH_WIDTH, D_MODEL), BRANCH_WIDTH ** -0.5),
        "b_norm_pre": gain((N_B, D_MODEL)),
        "b_norm_post": gain((N_B, D_MODEL)),
        "b_mem_norm": gain((N_B, D_MODEL)),
        "b_w_in": nrm((N_B, D_MODEL, B_IN_WIDTH), D_MODEL ** -0.5),
        "b_conv_w": nrm((N_B, CONFORMER_CONV_W, MIX_WIDTH), CONFORMER_CONV_W ** -0.5),
        "b_conv_b": nrm((N_B, MIX_WIDTH), 0.02),
        "b_ln_g": gain((N_B, MIX_WIDTH)),
        "b_ln_b": nrm((N_B, MIX_WIDTH), 0.02),
        "b_w_kv": nrm((N_B, D_MODEL, 2 * XA_WIDTH), D_MODEL ** -0.5),
        "b_w_out": nrm((N_B, BRANCH_WIDTH, D_MODEL), BRANCH_WIDTH ** -0.5),
    }


def reference(x_prompt, x_sample, mem_prompt, cache_mem_k, cache_mem_v, state_conv_a,
              state_conv_b, a_norm_pre, a_norm_post, a_mem_norm, a_w_in, a_conv_w, a_w_kv,
              a_w_out, b_norm_pre, b_norm_post, b_mem_norm, b_w_in, b_conv_w, b_conv_b,
              b_ln_g, b_ln_b, b_w_kv, b_w_out):
    y_p, y_s = x_prompt, x_sample
    n_prompt = x_prompt.shape[0]
    mk_p, mv_p, ca_p, cb_p, ca_s, cb_s = [], [], [], [], [], []
    for i in range(DEPTH):
        j = i // 2
        if i % 2 == 0:
            k_p, v_p = memory_kv(mem_prompt, a_mem_norm[j], a_w_kv[j])
            zero_state = jnp.zeros((n_prompt, SHORT_CONV_W - 1, MIX_WIDTH), x_prompt.dtype)
            y_p, st_p = short_conv_layer(y_p, zero_state, k_p, v_p, a_norm_pre[j],
                                         a_norm_post[j], a_w_in[j], a_conv_w[j], a_w_out[j])
            y_s, st_s = short_conv_layer(y_s, state_conv_a[j], cache_mem_k[i], cache_mem_v[i],
                                         a_norm_pre[j], a_norm_post[j], a_w_in[j],
                                         a_conv_w[j], a_w_out[j])
            ca_p.append(st_p)
            ca_s.append(st_s)
        else:
            k_p, v_p = memory_kv(mem_prompt, b_mem_norm[j], b_w_kv[j])
            zero_state = jnp.zeros((n_prompt, CONFORMER_CONV_W - 1, MIX_WIDTH), x_prompt.dtype)
            y_p, st_p = conformer_layer(y_p, zero_state, k_p, v_p, b_norm_pre[j],
                                        b_norm_post[j], b_w_in[j], b_conv_w[j], b_conv_b[j],
                                        b_ln_g[j], b_ln_b[j], b_w_out[j])
            y_s, st_s = conformer_layer(y_s, state_conv_b[j], cache_mem_k[i], cache_mem_v[i],
                                        b_norm_pre[j], b_norm_post[j], b_w_in[j], b_conv_w[j],
                                        b_conv_b[j], b_ln_g[j], b_ln_b[j], b_w_out[j])
            cb_p.append(st_p)
            cb_s.append(st_s)
        mk_p.append(k_p)
        mv_p.append(v_p)
    mem_k_prompt = jnp.stack(mk_p)
    mem_v_prompt = jnp.stack(mv_p)
    conv_a_prompt = jnp.stack(ca_p)
    conv_b_prompt = jnp.stack(cb_p)
    conv_a_sample = jnp.stack(ca_s)
    conv_b_sample = jnp.stack(cb_s)
    return (y_p, y_s, mem_k_prompt, mem_v_prompt, conv_a_prompt, conv_b_prompt,
            conv_a_sample, conv_b_sample)
```

```python
import jax
import jax.numpy as jnp
from jax import lax
from jax.experimental import pallas as pl
from jax.experimental.pallas import tpu as pltpu

F32 = jnp.float32
BF16 = jnp.bfloat16

D_MODEL = 1024
N_MEM = 256
HEADS = 4
HEAD_DIM = 128
XA = HEADS * HEAD_DIM
BRANCH = 2 * D_MODEL
MIX = BRANCH - XA
CONV_A = 3
CONV_B = 31
EPS = 1e-6
SCALE = HEAD_DIM ** -0.5

SUBLANES = 8
LANES = 128
CHUNK = 512
PROMPT_TILE = 256
SAMPLE_SEQS = SUBLANES
HDR_A = SUBLANES
HDR_B = 4 * SUBLANES
OFF_B = HDR_B - (CONV_B - 1)
CONV_ROWS = 64
VMEM_LIMIT = 56 * 1024 * 1024


def _dot(a, b):
    return jnp.dot(a, b, preferred_element_type=F32)


def _rms(x, g):
    ms = jnp.mean(x * x, axis=-1, keepdims=True)
    return x * lax.rsqrt(ms + EPS) * g


def _silu(z):
    return z * jax.nn.sigmoid(z)


def _layer_norm(c, g, b):
    mu = jnp.mean(c, axis=-1, keepdims=True)
    xc = c - mu
    var = jnp.mean(xc * xc, axis=-1, keepdims=True)
    return xc * lax.rsqrt(var + EPS) * g + b


def _softmax_pv(s, v):
    m = jnp.max(s, axis=-1, keepdims=True)
    p = jnp.exp(s - m)
    l = jnp.sum(p, axis=-1, keepdims=True)
    return _dot(p.astype(BF16), v) / l


def _prompt_attention(q, zx, k_ref, v_ref, br_ref):
    for h in range(HEADS):
        hs = slice(h * HEAD_DIM, (h + 1) * HEAD_DIM)
        qh = (q[:, hs] * SCALE).astype(BF16)
        s = lax.dot_general(qh, k_ref[:, hs], (((1,), (1,)), ((), ())),
                            preferred_element_type=F32)
        o = _softmax_pv(s, v_ref[:, hs])
        br_ref[:, MIX + h * HEAD_DIM:MIX + (h + 1) * HEAD_DIM] = (
            o * _silu(zx[:, hs])).astype(BF16)


def _finish(x, br_ref, wout_ref, gpost_ref, y_ref):
    out = _dot(br_ref[...], wout_ref[...])
    y_ref[...] = x + _rms(out, gpost_ref[...])


def _head_rows(h):
    return pl.ds(h, N_MEM, stride=HEADS)


def _memkv_kernel(mem_ref, g_ref, w_ref, k_ref, v_ref, kb_ref, vb_ref):
    mn = _rms(mem_ref[...], g_ref[...]).astype(BF16)
    kv = _dot(mn, w_ref[...])
    k = kv[:, :XA]
    v = kv[:, XA:]
    for h in range(HEADS):
        hs = slice(h * HEAD_DIM, (h + 1) * HEAD_DIM)
        k_ref[_head_rows(h), :] = k[:, hs]
        v_ref[_head_rows(h), :] = v[:, hs]
    kb_ref[...] = k.astype(BF16)
    vb_ref[...] = v.astype(BF16)


def _memory_kv(mem, gains, w_kv):
    nb = mem.shape[0]
    nl = w_kv.shape[0]
    out_f = jax.ShapeDtypeStruct((nl, nb, N_MEM * HEADS, HEAD_DIM), F32)
    out_b = jax.ShapeDtypeStruct((nl, nb, N_MEM, XA), BF16)
    f_spec = pl.BlockSpec((None, None, N_MEM * HEADS, HEAD_DIM), lambda l, b: (l, b, 0, 0))
    b_spec = pl.BlockSpec((None, None, N_MEM, XA), lambda l, b: (l, b, 0, 0))
    return pl.pallas_call(
        _memkv_kernel,
        out_shape=(out_f, out_f, out_b, out_b),
        grid=(nl, nb),
        in_specs=[
            pl.BlockSpec((None, N_MEM, D_MODEL), lambda l, b: (b, 0, 0)),
            pl.BlockSpec((None, 1, D_MODEL), lambda l, b: (l, 0, 0)),
            pl.BlockSpec((None, D_MODEL, 2 * XA), lambda l, b: (l, 0, 0)),
        ],
        out_specs=(f_spec, f_spec, b_spec, b_spec),
        compiler_params=pltpu.CompilerParams(
            dimension_semantics=("arbitrary", "arbitrary")),
        name="memory_kv",
    )(mem, gains, w_kv)


def _prompt_a_kernel(x_ref, k_ref, v_ref, gpre_ref, gpost_ref, win_ref, cw_ref, wout_ref,
                     y_ref, st_ref, cbuf, br_ref):
    tm = x_ref.shape[0]

    @pl.when(pl.program_id(1) == 0)
    def _():
        cbuf[0:HDR_A, :] = jnp.zeros((HDR_A, MIX), F32)

    x = x_ref[...]
    xn = _rms(x, gpre_ref[...]).astype(BF16)
    for j in range(MIX // CHUNK):
        c0 = j * CHUNK
        cs = slice(c0, c0 + CHUNK)
        hin = _dot(xn, win_ref[:, c0:c0 + CHUNK])
        cg = _dot(xn, win_ref[:, 2 * MIX + c0:2 * MIX + c0 + CHUNK])
        cbuf[HDR_A:HDR_A + tm, cs] = cg * hin
        conv = (cw_ref[2:3, cs] * cbuf[HDR_A:HDR_A + tm, cs]
                + cw_ref[1:2, cs] * cbuf[HDR_A - 1:HDR_A - 1 + tm, cs]
                + cw_ref[0:1, cs] * cbuf[HDR_A - 2:HDR_A - 2 + tm, cs])
        bg = _dot(xn, win_ref[:, MIX + c0:MIX + c0 + CHUNK])
        z = _dot(xn, win_ref[:, 3 * MIX + c0:3 * MIX + c0 + CHUNK])
        br_ref[:, cs] = (bg * conv * _silu(z)).astype(BF16)
    zx = _dot(xn, win_ref[:, 4 * MIX:4 * MIX + XA])
    q = _dot(xn, win_ref[:, 3 * MIX + BRANCH:])
    _prompt_attention(q, zx, k_ref, v_ref, br_ref)
    _finish(x, br_ref, wout_ref, gpost_ref, y_ref)
    st_ref[...] = cbuf[HDR_A + tm - (CONV_A - 1):HDR_A + tm, :]
    cbuf[0:HDR_A, :] = cbuf[tm:tm + HDR_A, :]


def _conv_b_block(cbuf, cw_ref, r0, rows, ls):
    out = None
    for b in range(SUBLANES):
        taps = [k for k in range(CONV_B) if (OFF_B + k) % SUBLANES == b]
        ext = rows if b == 0 else rows + SUBLANES
        part = None
        for k in taps:
            a0 = r0 + OFF_B + k - b
            term = cw_ref[k:k + 1, ls] * cbuf[a0:a0 + ext, ls]
            part = term if part is None else part + term
        part = part[b:b + rows, :]
        out = part if out is None else out + part
    return out


def _prompt_b_kernel(x_ref, k_ref, v_ref, gpre_ref, gpost_ref, win_ref, cw_ref, cb_ref,
                     lng_ref, lnb_ref, wout_ref, y_ref, st_ref, cbuf, cscr, br_ref):
    tm = x_ref.shape[0]

    @pl.when(pl.program_id(1) == 0)
    def _():
        cbuf[0:HDR_B, :] = jnp.zeros((HDR_B, MIX), F32)

    x = x_ref[...]
    xn = _rms(x, gpre_ref[...]).astype(BF16)
    for j in range(MIX // CHUNK):
        c0 = j * CHUNK
        val = _dot(xn, win_ref[:, c0:c0 + CHUNK])
        glu = _dot(xn, win_ref[:, MIX + c0:MIX + c0 + CHUNK])
        cbuf[HDR_B:HDR_B + tm, c0:c0 + CHUNK] = val * jax.nn.sigmoid(glu)
    for rb in range(tm // CONV_ROWS):
        r0 = rb * CONV_ROWS
        for lb in range(MIX // LANES):
            ls = slice(lb * LANES, (lb + 1) * LANES)
            cscr[r0:r0 + CONV_ROWS, ls] = (
                _conv_b_block(cbuf, cw_ref, r0, CONV_ROWS, ls) + cb_ref[:, ls])
    mixv = _silu(_layer_norm(cscr[...], lng_ref[...], lnb_ref[...]))
    for j in range(MIX // CHUNK):
        c0 = j * CHUNK
        z = _dot(xn, win_ref[:, 2 * MIX + c0:2 * MIX + c0 + CHUNK])
        br_ref[:, c0:c0 + CHUNK] = (mixv[:, c0:c0 + CHUNK] * _silu(z)).astype(BF16)
    zx = _dot(xn, win_ref[:, 3 * MIX:3 * MIX + XA])
    q = _dot(xn, win_ref[:, 2 * MIX + BRANCH:])
    _prompt_attention(q, zx, k_ref, v_ref, br_ref)
    _finish(x, br_ref, wout_ref, gpost_ref, y_ref)
    st_ref[...] = cbuf[HDR_B + tm - (CONV_B - 1):HDR_B + tm, :]
    cbuf[0:HDR_B, :] = cbuf[tm:tm + HDR_B, :]


def _resident(shape):
    zeros = (0,) * len(shape)
    return pl.BlockSpec(shape, lambda *_: zeros, pipeline_mode=pl.Buffered(1))


def _prompt_layer(kind, layer, x, kb, vb, weights):
    nb, t, d = x.shape
    tm = PROMPT_TILE
    wspecs = [_resident(w.shape) for w in weights]
    if kind == "a":
        kern = _prompt_a_kernel
        nstate = CONV_A - 1
        scratch = [pltpu.VMEM((HDR_A + tm, MIX), F32), pltpu.VMEM((tm, BRANCH), BF16)]
    else:
        kern = _prompt_b_kernel
        nstate = CONV_B - 1
        scratch = [pltpu.VMEM((HDR_B + tm, MIX), F32), pltpu.VMEM((tm, MIX), F32),
                   pltpu.VMEM((tm, BRANCH), BF16)]
    kv_spec = pl.BlockSpec((None, None, N_MEM, XA), lambda b, i: (layer, b, 0, 0))
    return pl.pallas_call(
        kern,
        out_shape=(jax.ShapeDtypeStruct((nb, t, d), F32),
                   jax.ShapeDtypeStruct((nb, nstate, MIX), F32)),
        grid=(nb, t // tm),
        in_specs=[pl.BlockSpec((None, tm, d), lambda b, i: (b, i, 0)), kv_spec, kv_spec] + wspecs,
        out_specs=(pl.BlockSpec((None, tm, d), lambda b, i: (b, i, 0)),
                   pl.BlockSpec((None, nstate, MIX), lambda b, i: (b, 0, 0))),
        scratch_shapes=scratch,
        compiler_params=pltpu.CompilerParams(
            dimension_semantics=("arbitrary", "arbitrary"),
            vmem_limit_bytes=VMEM_LIMIT),
        name="prompt_layer_" + kind,
    )(x, kb, vb, *weights)


def _sample_attention(q, zx, k_ref, v_ref, br_ref, nseq, ntok):
    rows = HEADS * ntok
    row_head = lax.broadcasted_iota(jnp.int32, (rows, XA), 0) // ntok
    lane_head = lax.broadcasted_iota(jnp.int32, (rows, XA), 1) // HEAD_DIM
    diag = row_head == lane_head
    for g in range(nseq):
        rs = slice(g * ntok, (g + 1) * ntok)
        kg = jnp.concatenate([k_ref[g, _head_rows(h), :] for h in range(HEADS)], axis=1)
        vg = jnp.concatenate([v_ref[g, _head_rows(h), :] for h in range(HEADS)], axis=1)
        qg = q[rs, :] * SCALE
        qd = jnp.where(diag, jnp.concatenate([qg] * HEADS, axis=0), 0.0).astype(BF16)
        s = lax.dot_general(qd, kg.astype(BF16), (((1,), (1,)), ((), ())),
                            preferred_element_type=F32)
        od = _softmax_pv(s, vg.astype(BF16))
        o = jnp.concatenate(
            [od[h * ntok:(h + 1) * ntok, h * HEAD_DIM:(h + 1) * HEAD_DIM] for h in range(HEADS)],
            axis=1)
        br_ref[rs, MIX:] = (o * _silu(zx[rs, :])).astype(BF16)


def _sample_a_kernel(x_ref, k_ref, v_ref, st_ref, gpre_ref, gpost_ref, win_ref, cw_ref, wout_ref,
                     y_ref, nst_ref, cbuf, br_ref):
    nseq = st_ref.shape[0]
    ntok = x_ref.shape[0] // nseq
    per = HDR_A + ntok
    x = x_ref[...]
    xn = _rms(x, gpre_ref[...]).astype(BF16)
    for g in range(nseq):
        cbuf[g * per + HDR_A - (CONV_A - 1):g * per + HDR_A, :] = st_ref[g]
    for j in range(MIX // CHUNK):
        c0 = j * CHUNK
        cs = slice(c0, c0 + CHUNK)
        hin = _dot(xn, win_ref[:, c0:c0 + CHUNK])
        cg = _dot(xn, win_ref[:, 2 * MIX + c0:2 * MIX + c0 + CHUNK])
        u = cg * hin
        convs = []
        for g in range(nseq):
            b0 = g * per + HDR_A
            cbuf[b0:b0 + ntok, cs] = u[g * ntok:(g + 1) * ntok, :]
            convs.append(cw_ref[2:3, cs] * cbuf[b0:b0 + ntok, cs]
                         + cw_ref[1:2, cs] * cbuf[b0 - 1:b0 - 1 + ntok, cs]
                         + cw_ref[0:1, cs] * cbuf[b0 - 2:b0 - 2 + ntok, cs])
        conv = jnp.concatenate(convs, axis=0)
        bg = _dot(xn, win_ref[:, MIX + c0:MIX + c0 + CHUNK])
        z = _dot(xn, win_ref[:, 3 * MIX + c0:3 * MIX + c0 + CHUNK])
        br_ref[:, cs] = (bg * conv * _silu(z)).astype(BF16)
    for g in range(nseq):
        b0 = g * per + HDR_A
        nst_ref[g] = cbuf[b0 + ntok - (CONV_A - 1):b0 + ntok, :]
    zx = _dot(xn, win_ref[:, 4 * MIX:4 * MIX + XA])
    q = _dot(xn, win_ref[:, 3 * MIX + BRANCH:])
    _sample_attention(q, zx, k_ref, v_ref, br_ref, nseq, ntok)
    _finish(x, br_ref, wout_ref, gpost_ref, y_ref)


def _sample_b_kernel(x_ref, k_ref, v_ref, st_ref, gpre_ref, gpost_ref, win_ref, cw_ref, cb_ref,
                     lng_ref, lnb_ref, wout_ref, *rest):
    y_ref, nst_ref, cbuf, tscr, br_ref = rest[-5:]
    nstate = CONV_B - 1
    nseq = st_ref.shape[1]
    ntok = x_ref.shape[0] // nseq
    nlb = MIX // LANES
    x = x_ref[...]
    xn = _rms(x, gpre_ref[...]).astype(BF16)
    for i in range(nstate):
        cbuf[i * nseq:(i + 1) * nseq, :] = st_ref[i]
    for j in range(MIX // CHUNK):
        c0 = j * CHUNK
        val = _dot(xn, win_ref[:, c0:c0 + CHUNK])
        glu = _dot(xn, win_ref[:, MIX + c0:MIX + c0 + CHUNK])
        u = val * jax.nn.sigmoid(glu)
        for lb in range(CHUNK // LANES):
            tscr[j * (CHUNK // LANES) + lb] = u[:, lb * LANES:(lb + 1) * LANES]
    for t in range(ntok):
        r0 = (nstate + t) * nseq
        for lb in range(nlb):
            cbuf[r0:r0 + nseq, lb * LANES:(lb + 1) * LANES] = (
                tscr[lb, pl.ds(t, nseq, stride=ntok), :])
    for i in range(nstate):
        nst_ref[i] = cbuf[(ntok + i) * nseq:(ntok + i + 1) * nseq, :]
    for t in range(ntok):
        acc = cb_ref[...] + cw_ref[0:1, :] * cbuf[t * nseq:(t + 1) * nseq, :]
        for k in range(1, CONV_B):
            acc = acc + cw_ref[k:k + 1, :] * cbuf[(t + k) * nseq:(t + k + 1) * nseq, :]
        mix_t = _silu(_layer_norm(acc, lng_ref[...], lnb_ref[...]))
        for lb in range(nlb):
            tscr[lb, pl.ds(t, nseq, stride=ntok), :] = mix_t[:, lb * LANES:(lb + 1) * LANES]
    for j in range(MIX // CHUNK):
        c0 = j * CHUNK
        z = _dot(xn, win_ref[:, 2 * MIX + c0:2 * MIX + c0 + CHUNK])
        mixv = jnp.concatenate(
            [tscr[j * (CHUNK // LANES) + lb] for lb in range(CHUNK // LANES)], axis=1)
        br_ref[:, c0:c0 + CHUNK] = (mixv * _silu(z)).astype(BF16)
    zx = _dot(xn, win_ref[:, 3 * MIX:3 * MIX + XA])
    q = _dot(xn, win_ref[:, 2 * MIX + BRANCH:])
    _sample_attention(q, zx, k_ref, v_ref, br_ref, nseq, ntok)
    _finish(x, br_ref, wout_ref, gpost_ref, y_ref)


def _sample_layer(kind, layer, x, cache_k, cache_v, state, sidx, weights, ntok, prev_state=None):
    rows, d = x.shape
    nseq = SAMPLE_SEQS
    tm = nseq * ntok
    nseq_total = rows // ntok
    wspecs = [_resident(w.shape) for w in weights]
    kv_spec = pl.BlockSpec((None, nseq, N_MEM * HEADS, HEAD_DIM), lambda i: (layer, i, 0, 0))
    x_spec = pl.BlockSpec((tm, d), lambda i: (i, 0))
    operands = [x, cache_k, cache_v, state]
    aliases = {}
    if kind == "a":
        kern = _sample_a_kernel
        nstate = CONV_A - 1
        st_in = pl.BlockSpec((None, nseq, nstate, MIX), lambda i: (sidx, i, 0, 0))
        st_shape = jax.ShapeDtypeStruct((nseq_total, nstate, MIX), F32)
        st_out = pl.BlockSpec((nseq, nstate, MIX), lambda i: (i, 0, 0))
        scratch = [pltpu.VMEM((nseq * (HDR_A + ntok), MIX), F32), pltpu.VMEM((tm, BRANCH), BF16)]
        extra_specs = []
    else:
        kern = _sample_b_kernel
        nstate = CONV_B - 1
        st_in = pl.BlockSpec((None, nstate, nseq, MIX), lambda i: (sidx, 0, i, 0))
        st_shape = jax.ShapeDtypeStruct(state.shape, F32)
        st_out = pl.BlockSpec((None, nstate, nseq, MIX), lambda i: (sidx, 0, i, 0))
        scratch = [pltpu.VMEM(((nstate + ntok) * nseq, MIX), F32),
                   pltpu.VMEM((MIX // LANES, tm, LANES), F32),
                   pltpu.VMEM((tm, BRANCH), BF16)]
        extra_specs = []
        if prev_state is not None:
            extra_specs = [pl.BlockSpec(memory_space=pl.ANY)]
            aliases = {4 + len(weights): 1}
    in_specs = [x_spec, kv_spec, kv_spec, st_in] + wspecs + extra_specs
    operands = operands + list(weights) + ([prev_state] if extra_specs else [])
    return pl.pallas_call(
        kern,
        out_shape=(jax.ShapeDtypeStruct((rows, d), F32), st_shape),
        grid=(rows // tm,),
        in_specs=in_specs,
        out_specs=(x_spec, st_out),
        scratch_shapes=scratch,
        input_output_aliases=aliases,
        compiler_params=pltpu.CompilerParams(
            dimension_semantics=("arbitrary",),
            vmem_limit_bytes=VMEM_LIMIT),
        name="sample_layer_" + kind,
    )(*operands)


def kernel(x_prompt, x_sample, mem_prompt, cache_mem_k, cache_mem_v, state_conv_a, state_conv_b, a_norm_pre, a_norm_post, a_mem_norm, a_w_in, a_conv_w, a_w_kv, a_w_out, b_norm_pre, b_norm_post, b_mem_norm, b_w_in, b_conv_w, b_conv_b, b_ln_g, b_ln_b, b_w_kv, b_w_out):
    depth = cache_mem_k.shape[0]
    n_prompt = x_prompt.shape[0]
    n_sample, ntok, d = x_sample.shape

    def row(v):
        return v.reshape(1, -1)

    w_kv = jnp.stack([(a_w_kv if i % 2 == 0 else b_w_kv)[i // 2] for i in range(depth)]).astype(BF16)
    g_mem = jnp.stack([(a_mem_norm if i % 2 == 0 else b_mem_norm)[i // 2] for i in range(depth)])
    k_p, v_p, kb_p, vb_p = _memory_kv(mem_prompt, g_mem.reshape(depth, 1, d), w_kv)

    cache_k = cache_mem_k.reshape(depth, n_sample, N_MEM * HEADS, HEAD_DIM)
    cache_v = cache_mem_v.reshape(depth, n_sample, N_MEM * HEADS, HEAD_DIM)
    state_b = jnp.transpose(state_conv_b, (0, 2, 1, 3))
    y_p = x_prompt
    y_s = x_sample.reshape(n_sample * ntok, d)
    ca_p, cb_p, ca_s = [], [], []
    cb_s = None
    for i in range(depth):
        j = i // 2
        if i % 2 == 0:
            weights = (row(a_norm_pre[j]), row(a_norm_post[j]), a_w_in[j].astype(BF16),
                       a_conv_w[j], a_w_out[j].astype(BF16))
            y_p, st_p = _prompt_layer("a", i, y_p, kb_p, vb_p, weights)
            y_s, st_s = _sample_layer("a", i, y_s, cache_k, cache_v, state_conv_a, j, weights, ntok)
            ca_p.append(st_p)
            ca_s.append(st_s)
        else:
            weights = (row(b_norm_pre[j]), row(b_norm_post[j]), b_w_in[j].astype(BF16),
                       b_conv_w[j], row(b_conv_b[j]), row(b_ln_g[j]), row(b_ln_b[j]),
                       b_w_out[j].astype(BF16))
            y_p, st_p = _prompt_layer("b", i, y_p, kb_p, vb_p, weights)
            y_s, cb_s = _sample_layer("b", i, y_s, cache_k, cache_v, state_b, j, weights, ntok,
                                      prev_state=cb_s)
            cb_p.append(st_p)
    kv_shape = (depth, n_prompt, N_MEM, HEADS, HEAD_DIM)
    return (y_p, y_s.reshape(n_sample, ntok, d), k_p.reshape(kv_shape), v_p.reshape(kv_shape),
            jnp.stack(ca_p), jnp.stack(cb_p), jnp.stack(ca_s),
            jnp.transpose(cb_s, (0, 2, 1, 3)))
```

```python
import jax
import jax.numpy as jnp
from jax import lax
from jax.experimental import pallas as pl
from jax.experimental.pallas import tpu as pltpu

F32 = jnp.float32
BF16 = jnp.bfloat16

D_MODEL = 1024
N_MEM = 256
HEADS = 4
HEAD_DIM = 128
XA = HEADS * HEAD_DIM
BRANCH = 2 * D_MODEL
MIX = BRANCH - XA
CONV_A = 3
CONV_B = 31
EPS = 1e-6
SCALE = HEAD_DIM ** -0.5

SUBLANES = 8
LANES = 128
CHUNK = 512
PROMPT_TILE = 256
SAMPLE_SEQS = SUBLANES
SAMPLE_TILE_SEQS = 2 * SAMPLE_SEQS
HDR_A = SUBLANES
HDR_B = 4 * SUBLANES
OFF_B = HDR_B - (CONV_B - 1)
CONV_ROWS = 64
ROW_PITCH = 2
VMEM_LIMIT = 58 * 1024 * 1024


def _dot(a, b):
    return jnp.dot(a, b, preferred_element_type=F32)


def _rms(x, g):
    ms = jnp.mean(x * x, axis=-1, keepdims=True)
    return x * lax.rsqrt(ms + EPS) * g


def _silu(z):
    return z * jax.nn.sigmoid(z)


def _layer_norm(c, g, b):
    mu = jnp.mean(c, axis=-1, keepdims=True)
    xc = c - mu
    var = jnp.mean(xc * xc, axis=-1, keepdims=True)
    return xc * lax.rsqrt(var + EPS) * g + b


def _softmax_pv(s, v):
    m = jnp.max(s, axis=-1, keepdims=True)
    p = jnp.exp(s - m)
    l = jnp.sum(p, axis=-1, keepdims=True)
    return _dot(p.astype(BF16), v) / l


def _prompt_attention(q, zx, k_ref, v_ref, br_ref):
    for h in range(HEADS):
        hs = slice(h * HEAD_DIM, (h + 1) * HEAD_DIM)
        qh = (q[:, hs] * SCALE).astype(BF16)
        s = lax.dot_general(qh, k_ref[:, hs], (((1,), (1,)), ((), ())),
                            preferred_element_type=F32)
        o = _softmax_pv(s, v_ref[:, hs])
        br_ref[:, MIX + h * HEAD_DIM:MIX + (h + 1) * HEAD_DIM] = (
            o * _silu(zx[:, hs])).astype(BF16)


def _finish(x, br_ref, wout_ref, gpost_ref, y_ref):
    out = _dot(br_ref[...], wout_ref[...])
    y_ref[...] = x + _rms(out, gpost_ref[...])


def _head_rows(h):
    return pl.ds(h, N_MEM, stride=HEADS)


def _rows(start, n):
    return pl.ds(ROW_PITCH * start, n, stride=ROW_PITCH)


def _memkv_kernel(mem_ref, g_ref, w_ref, k_ref, v_ref, kb_ref, vb_ref):
    mn = _rms(mem_ref[...], g_ref[...]).astype(BF16)
    kv = _dot(mn, w_ref[...])
    k = kv[:, :XA]
    v = kv[:, XA:]
    for h in range(HEADS):
        hs = slice(h * HEAD_DIM, (h + 1) * HEAD_DIM)
        k_ref[_head_rows(h), :] = k[:, hs]
        v_ref[_head_rows(h), :] = v[:, hs]
    kb_ref[...] = k.astype(BF16)
    vb_ref[...] = v.astype(BF16)


def _memory_kv(mem, gains, w_kv):
    nb = mem.shape[0]
    nl = w_kv.shape[0]
    out_f = jax.ShapeDtypeStruct((nl, nb, N_MEM * HEADS, HEAD_DIM), F32)
    out_b = jax.ShapeDtypeStruct((nl, nb, N_MEM, XA), BF16)
    f_spec = pl.BlockSpec((None, None, N_MEM * HEADS, HEAD_DIM), lambda l, b: (l, b, 0, 0))
    b_spec = pl.BlockSpec((None, None, N_MEM, XA), lambda l, b: (l, b, 0, 0))
    return pl.pallas_call(
        _memkv_kernel,
        out_shape=(out_f, out_f, out_b, out_b),
        grid=(nl, nb),
        in_specs=[
            pl.BlockSpec((None, N_MEM, D_MODEL), lambda l, b: (b, 0, 0)),
            pl.BlockSpec((None, 1, D_MODEL), lambda l, b: (l, 0, 0)),
            pl.BlockSpec((None, D_MODEL, 2 * XA), lambda l, b: (l, 0, 0)),
        ],
        out_specs=(f_spec, f_spec, b_spec, b_spec),
        compiler_params=pltpu.CompilerParams(
            dimension_semantics=("arbitrary", "arbitrary")),
        name="memory_kv",
    )(mem, gains, w_kv)


def _prompt_a_kernel(x_ref, k_ref, v_ref, gpre_ref, gpost_ref, win_ref, cw_ref, wout_ref,
                     y_ref, st_ref, cbuf, br_ref):
    tm = x_ref.shape[0]
    nsl = CHUNK // LANES

    @pl.when(pl.program_id(1) == 0)
    def _():
        for sl in range(MIX // LANES):
            cbuf[sl, _rows(0, HDR_A), :] = jnp.zeros((HDR_A, LANES), F32)

    x = x_ref[...]
    xn = _rms(x, gpre_ref[...]).astype(BF16)

    def chunk(j):
        c0 = j * CHUNK
        hin = _dot(xn, win_ref[:, c0:c0 + CHUNK])
        cg = _dot(xn, win_ref[:, 2 * MIX + c0:2 * MIX + c0 + CHUNK])
        u = cg * hin
        bg = _dot(xn, win_ref[:, MIX + c0:MIX + c0 + CHUNK])
        z = _dot(xn, win_ref[:, 3 * MIX + c0:3 * MIX + c0 + CHUNK])
        gate = bg * _silu(z)
        st_ref[:, c0:c0 + CHUNK] = u[tm - (CONV_A - 1):, :]
        for lb in range(nsl):
            sl = j * nsl + lb
            ls = slice(c0 + lb * LANES, c0 + (lb + 1) * LANES)
            ul = u[:, lb * LANES:(lb + 1) * LANES]
            cbuf[sl, _rows(HDR_A, tm), :] = ul
            conv = (cw_ref[2:3, ls] * ul
                    + cw_ref[1:2, ls] * cbuf[sl, _rows(HDR_A - 1, tm), :]
                    + cw_ref[0:1, ls] * cbuf[sl, _rows(HDR_A - 2, tm), :])
            br_ref[:, ls] = (gate[:, lb * LANES:(lb + 1) * LANES] * conv).astype(BF16)
            cbuf[sl, _rows(0, HDR_A), :] = ul[tm - HDR_A:, :]

    chunk(0)
    zx = _dot(xn, win_ref[:, 4 * MIX:4 * MIX + XA])
    q = _dot(xn, win_ref[:, 3 * MIX + BRANCH:])
    _prompt_attention(q, zx, k_ref, v_ref, br_ref)
    chunk(1)
    chunk(2)
    _finish(x, br_ref, wout_ref, gpost_ref, y_ref)


def _prompt_b_kernel(x_ref, k_ref, v_ref, gpre_ref, gpost_ref, win_ref, cw_ref, cb_ref,
                     lng_ref, lnb_ref, wout_ref, y_ref, st_ref, cbuf, cscr, br_ref):
    tm = x_ref.shape[0]
    nsl = CHUNK // LANES

    @pl.when(pl.program_id(1) == 0)
    def _():
        for sl in range(MIX // LANES):
            cbuf[sl, _rows(0, HDR_B), :] = jnp.zeros((HDR_B, LANES), F32)

    x = x_ref[...]
    xn = _rms(x, gpre_ref[...]).astype(BF16)
    for j in range(MIX // CHUNK):
        c0 = j * CHUNK
        val = _dot(xn, win_ref[:, c0:c0 + CHUNK])
        glu = _dot(xn, win_ref[:, MIX + c0:MIX + c0 + CHUNK])
        u = val * jax.nn.sigmoid(glu)
        for lb in range(nsl):
            cbuf[j * nsl + lb, _rows(HDR_B, tm), :] = u[:, lb * LANES:(lb + 1) * LANES]
    for sl in range(MIX // LANES):
        ls = slice(sl * LANES, (sl + 1) * LANES)
        for rb in range(tm // CONV_ROWS):
            r0 = rb * CONV_ROWS + OFF_B
            acc = cb_ref[:, ls] + cw_ref[0:1, ls] * cbuf[sl, _rows(r0, CONV_ROWS), :]
            for k in range(1, CONV_B):
                acc = acc + cw_ref[k:k + 1, ls] * cbuf[sl, _rows(r0 + k, CONV_ROWS), :]
            cscr[rb * CONV_ROWS:(rb + 1) * CONV_ROWS, ls] = acc
        tail = cbuf[sl, _rows(tm, HDR_B), :]
        st_ref[:, ls] = tail[OFF_B:, :]
        cbuf[sl, _rows(0, HDR_B), :] = tail
    mixv = _silu(_layer_norm(cscr[...], lng_ref[...], lnb_ref[...]))
    for j in range(MIX // CHUNK):
        c0 = j * CHUNK
        z = _dot(xn, win_ref[:, 2 * MIX + c0:2 * MIX + c0 + CHUNK])
        br_ref[:, c0:c0 + CHUNK] = (mixv[:, c0:c0 + CHUNK] * _silu(z)).astype(BF16)
    zx = _dot(xn, win_ref[:, 3 * MIX:3 * MIX + XA])
    q = _dot(xn, win_ref[:, 2 * MIX + BRANCH:])
    _prompt_attention(q, zx, k_ref, v_ref, br_ref)
    _finish(x, br_ref, wout_ref, gpost_ref, y_ref)


def _resident(w, j):
    idx = (j,) + (0,) * (w.ndim - 1)
    return pl.BlockSpec((None,) + w.shape[1:], lambda *_: idx, pipeline_mode=pl.Buffered(1))


def _prompt_layer(kind, layer, j, x, kb, vb, weights):
    nb, t, d = x.shape
    tm = PROMPT_TILE
    wspecs = [_resident(w, j) for w in weights]
    if kind == "a":
        kern = _prompt_a_kernel
        nstate = CONV_A - 1
        scratch = [pltpu.VMEM((MIX // LANES, ROW_PITCH * (HDR_A + tm), LANES), F32),
                   pltpu.VMEM((tm, BRANCH), BF16)]
    else:
        kern = _prompt_b_kernel
        nstate = CONV_B - 1
        scratch = [pltpu.VMEM((MIX // LANES, ROW_PITCH * (HDR_B + tm), LANES), F32),
                   pltpu.VMEM((tm, MIX), F32), pltpu.VMEM((tm, BRANCH), BF16)]
    kv_spec = pl.BlockSpec((None, None, N_MEM, XA), lambda b, i: (layer, b, 0, 0))
    return pl.pallas_call(
        kern,
        out_shape=(jax.ShapeDtypeStruct((nb, t, d), F32),
                   jax.ShapeDtypeStruct((nb, nstate, MIX), F32)),
        grid=(nb, t // tm),
        in_specs=[pl.BlockSpec((None, tm, d), lambda b, i: (b, i, 0)), kv_spec, kv_spec] + wspecs,
        out_specs=(pl.BlockSpec((None, tm, d), lambda b, i: (b, i, 0)),
                   pl.BlockSpec((None, nstate, MIX), lambda b, i: (b, 0, 0))),
        scratch_shapes=scratch,
        compiler_params=pltpu.CompilerParams(
            dimension_semantics=("arbitrary", "arbitrary"),
            vmem_limit_bytes=VMEM_LIMIT),
        name="prompt_layer_" + kind,
    )(x, kb, vb, *weights)


def _sample_attention(q_ref, zx_ref, k_ref, v_ref, br_ref, r0, nseq, ntok):
    rows = HEADS * ntok
    row_head = lax.broadcasted_iota(jnp.int32, (rows, XA), 0) // ntok
    lane_head = lax.broadcasted_iota(jnp.int32, (rows, XA), 1) // HEAD_DIM
    diag = row_head == lane_head
    pair = []
    for g in range(nseq):
        kg = jnp.concatenate([k_ref[g, _head_rows(h), :] for h in range(HEADS)], axis=1)
        vg = jnp.concatenate([v_ref[g, _head_rows(h), :] for h in range(HEADS)], axis=1)
        qg = q_ref[pl.ds(r0 + g * ntok, ntok), :]
        qd = jnp.where(diag, jnp.concatenate([qg] * HEADS, axis=0), 0.0).astype(BF16)
        s = lax.dot_general(qd, kg.astype(BF16), (((1,), (1,)), ((), ())),
                            preferred_element_type=F32)
        od = _softmax_pv(s, vg.astype(BF16))
        pair.append(jnp.concatenate(
            [od[h * ntok:(h + 1) * ntok, h * HEAD_DIM:(h + 1) * HEAD_DIM] for h in range(HEADS)],
            axis=1))
        if g % 2 == 1:
            rs = pl.ds(r0 + (g - 1) * ntok, 2 * ntok)
            o = jnp.concatenate(pair, axis=0)
            br_ref[rs, MIX:] = (o * zx_ref[rs, :]).astype(BF16)
            pair = []


def _sample_a_kernel(x_ref, k_ref, v_ref, st_ref, gpre_ref, gpost_ref, win_ref, cw_ref, wout_ref,
                     y_ref, nst_ref, cbuf, q_ref, zx_ref, br_ref):
    nseq = st_ref.shape[0]
    ntok = x_ref.shape[0] // nseq
    nsub = k_ref.shape[0]
    per = HDR_A + ntok
    j = pl.program_id(1)

    @pl.when(j == 0)
    def _():
        xn = _rms(x_ref[...], gpre_ref[...]).astype(BF16)
        for g in range(nseq):
            cbuf[g * per + HDR_A - (CONV_A - 1):g * per + HDR_A, :] = st_ref[g]
        for c in range(MIX // CHUNK):
            c0 = c * CHUNK
            cs = slice(c0, c0 + CHUNK)
            hin = _dot(xn, win_ref[:, c0:c0 + CHUNK])
            cg = _dot(xn, win_ref[:, 2 * MIX + c0:2 * MIX + c0 + CHUNK])
            u = cg * hin
            convs = []
            for g in range(nseq):
                b0 = g * per + HDR_A
                cbuf[b0:b0 + ntok, cs] = u[g * ntok:(g + 1) * ntok, :]
                convs.append(cw_ref[2:3, cs] * cbuf[b0:b0 + ntok, cs]
                             + cw_ref[1:2, cs] * cbuf[b0 - 1:b0 - 1 + ntok, cs]
                             + cw_ref[0:1, cs] * cbuf[b0 - 2:b0 - 2 + ntok, cs])
            conv = jnp.concatenate(convs, axis=0)
            bg = _dot(xn, win_ref[:, MIX + c0:MIX + c0 + CHUNK])
            z = _dot(xn, win_ref[:, 3 * MIX + c0:3 * MIX + c0 + CHUNK])
            br_ref[:, cs] = (bg * conv * _silu(z)).astype(BF16)
        for g in range(nseq):
            b0 = g * per + HDR_A
            nst_ref[g] = cbuf[b0 + ntok - (CONV_A - 1):b0 + ntok, :]
        zx_ref[...] = _silu(_dot(xn, win_ref[:, 4 * MIX:4 * MIX + XA]))
        q_ref[...] = _dot(xn, win_ref[:, 3 * MIX + BRANCH:]) * SCALE

    r0 = pl.multiple_of(j * (nsub * ntok), nsub * ntok)
    _sample_attention(q_ref, zx_ref, k_ref, v_ref, br_ref, r0, nsub, ntok)

    @pl.when(j == pl.num_programs(1) - 1)
    def _():
        _finish(x_ref[...], br_ref, wout_ref, gpost_ref, y_ref)


def _sample_b_kernel(x_ref, k_ref, v_ref, st_ref, gpre_ref, gpost_ref, win_ref, cw_ref, cb_ref,
                     lng_ref, lnb_ref, wout_ref, *rest):
    y_ref, nst_ref, ubuf, cscr, tscr, q_ref, zx_ref, br_ref = rest[-8:]
    nstate = CONV_B - 1
    nseq = st_ref.shape[1]
    ntok = x_ref.shape[0] // nseq
    nsub = k_ref.shape[0]
    nlb = MIX // LANES
    nsl = CHUNK // LANES
    j = pl.program_id(1)

    @pl.when(j == 0)
    def _():
        xn = _rms(x_ref[...], gpre_ref[...]).astype(BF16)
        for c in range(MIX // CHUNK):
            c0 = c * CHUNK
            val = _dot(xn, win_ref[:, c0:c0 + CHUNK])
            glu = _dot(xn, win_ref[:, MIX + c0:MIX + c0 + CHUNK])
            u = val * jax.nn.sigmoid(glu)
            for lb in range(nsl):
                tscr[c * nsl + lb] = u[:, lb * LANES:(lb + 1) * LANES]
        for t in range(ntok):
            for lb in range(nlb):
                ubuf[t * nseq:(t + 1) * nseq, lb * LANES:(lb + 1) * LANES] = (
                    tscr[lb, pl.ds(t, nseq, stride=ntok), :])

        def full(i, cs):
            if i < nstate:
                return st_ref[i, :, cs]
            return ubuf[(i - nstate) * nseq:(i - nstate + 1) * nseq, cs]

        for i in range(nstate):
            nst_ref[i] = full(ntok + i, slice(None))
        for t in range(ntok):
            for c in range(MIX // CHUNK):
                cs = slice(c * CHUNK, (c + 1) * CHUNK)
                acc = cb_ref[:, cs] + cw_ref[0:1, cs] * full(t, cs)
                for k in range(1, CONV_B):
                    acc = acc + cw_ref[k:k + 1, cs] * full(t + k, cs)
                cscr[t * nseq:(t + 1) * nseq, cs] = acc
        for t in range(ntok):
            mix_t = _silu(_layer_norm(cscr[t * nseq:(t + 1) * nseq, :], lng_ref[...], lnb_ref[...]))
            for lb in range(nlb):
                tscr[lb, pl.ds(t, nseq, stride=ntok), :] = mix_t[:, lb * LANES:(lb + 1) * LANES]
        for c in range(MIX // CHUNK):
            c0 = c * CHUNK
            z = _dot(xn, win_ref[:, 2 * MIX + c0:2 * MIX + c0 + CHUNK])
            mixv = jnp.concatenate([tscr[c * nsl + lb] for lb in range(nsl)], axis=1)
            br_ref[:, c0:c0 + CHUNK] = (mixv * _silu(z)).astype(BF16)
        zx_ref[...] = _silu(_dot(xn, win_ref[:, 3 * MIX:3 * MIX + XA]))
        q_ref[...] = _dot(xn, win_ref[:, 2 * MIX + BRANCH:]) * SCALE

    r0 = pl.multiple_of(j * (nsub * ntok), nsub * ntok)
    _sample_attention(q_ref, zx_ref, k_ref, v_ref, br_ref, r0, nsub, ntok)

    @pl.when(j == pl.num_programs(1) - 1)
    def _():
        _finish(x_ref[...], br_ref, wout_ref, gpost_ref, y_ref)


def _sample_layer(kind, layer, x, cache_k, cache_v, state, sidx, weights, ntok, prev_state=None):
    rows, d = x.shape
    nseq = SAMPLE_TILE_SEQS
    nsub = SAMPLE_SEQS
    tm = nseq * ntok
    nj = nseq // nsub
    nseq_total = rows // ntok
    wspecs = [_resident(w, sidx) for w in weights]
    kv_spec = pl.BlockSpec((None, nsub, N_MEM * HEADS, HEAD_DIM),
                           lambda i, j: (layer, i * nj + j, 0, 0))
    x_spec = pl.BlockSpec((tm, d), lambda i, j: (i, 0))
    attn_scratch = [pltpu.VMEM((tm, XA), F32), pltpu.VMEM((tm, XA), F32),
                    pltpu.VMEM((tm, BRANCH), BF16)]
    operands = [x, cache_k, cache_v, state]
    aliases = {}
    extra_specs = []
    if kind == "a":
        kern = _sample_a_kernel
        nstate = CONV_A - 1
        st_in = pl.BlockSpec((None, nseq, nstate, MIX), lambda i, j: (sidx, i, 0, 0))
        st_shape = jax.ShapeDtypeStruct((nseq_total, nstate, MIX), F32)
        st_out = pl.BlockSpec((nseq, nstate, MIX), lambda i, j: (i, 0, 0))
        scratch = [pltpu.VMEM((nseq * (HDR_A + ntok), MIX), F32)] + attn_scratch
    else:
        kern = _sample_b_kernel
        nstate = CONV_B - 1
        st_in = pl.BlockSpec((None, nstate, nseq, MIX), lambda i, j: (sidx, 0, i, 0),
                             pipeline_mode=pl.Buffered(1))
        st_shape = jax.ShapeDtypeStruct(state.shape, F32)
        st_out = pl.BlockSpec((None, nstate, nseq, MIX), lambda i, j: (sidx, 0, i, 0))
        scratch = [pltpu.VMEM((tm, MIX), F32),
                   pltpu.VMEM((tm, MIX), F32),
                   pltpu.VMEM((MIX // LANES, tm, LANES), F32)] + attn_scratch
        if prev_state is not None:
            extra_specs = [pl.BlockSpec(memory_space=pl.ANY)]
            aliases = {4 + len(weights): 1}
    in_specs = [x_spec, kv_spec, kv_spec, st_in] + wspecs + extra_specs
    operands = operands + list(weights) + ([prev_state] if extra_specs else [])
    return pl.pallas_call(
        kern,
        out_shape=(jax.ShapeDtypeStruct((rows, d), F32), st_shape),
        grid=(rows // tm, nj),
        in_specs=in_specs,
        out_specs=(x_spec, st_out),
        scratch_shapes=scratch,
        input_output_aliases=aliases,
        compiler_params=pltpu.CompilerParams(
            dimension_semantics=("arbitrary", "arbitrary"),
            vmem_limit_bytes=VMEM_LIMIT),
        name="sample_layer_" + kind,
    )(*operands)


def kernel(x_prompt, x_sample, mem_prompt, cache_mem_k, cache_mem_v, state_conv_a, state_conv_b, a_norm_pre, a_norm_post, a_mem_norm, a_w_in, a_conv_w, a_w_kv, a_w_out, b_norm_pre, b_norm_post, b_mem_norm, b_w_in, b_conv_w, b_conv_b, b_ln_g, b_ln_b, b_w_kv, b_w_out):
    depth = cache_mem_k.shape[0]
    n_prompt = x_prompt.shape[0]
    n_sample, ntok, d = x_sample.shape

    def rows3(v):
        return v.reshape(v.shape[0], 1, v.shape[1])

    w_a = (rows3(a_norm_pre), rows3(a_norm_post), a_w_in.astype(BF16), a_conv_w,
           a_w_out.astype(BF16))
    w_b = (rows3(b_norm_pre), rows3(b_norm_post), b_w_in.astype(BF16), b_conv_w,
           rows3(b_conv_b), rows3(b_ln_g), rows3(b_ln_b), b_w_out.astype(BF16))

    w_kv = jnp.stack([(a_w_kv if i % 2 == 0 else b_w_kv)[i // 2] for i in range(depth)]).astype(BF16)
    g_mem = jnp.stack([(a_mem_norm if i % 2 == 0 else b_mem_norm)[i // 2] for i in range(depth)])
    k_p, v_p, kb_p, vb_p = _memory_kv(mem_prompt, g_mem.reshape(depth, 1, d), w_kv)

    cache_k = cache_mem_k.reshape(depth, n_sample, N_MEM * HEADS, HEAD_DIM)
    cache_v = cache_mem_v.reshape(depth, n_sample, N_MEM * HEADS, HEAD_DIM)
    state_b = jnp.transpose(state_conv_b, (0, 2, 1, 3))
    y_p = x_prompt
    y_s = x_sample.reshape(n_sample * ntok, d)
    ca_p, cb_p, ca_s = [], [], []
    cb_s = None
    for i in range(depth):
        j = i // 2
        if i % 2 == 0:
            y_p, st_p = _prompt_layer("a", i, j, y_p, kb_p, vb_p, w_a)
            y_s, st_s = _sample_layer("a", i, y_s, cache_k, cache_v, state_conv_a, j, w_a, ntok)
            ca_p.append(st_p)
            ca_s.append(st_s)
        else:
            y_p, st_p = _prompt_layer("b", i, j, y_p, kb_p, vb_p, w_b)
            y_s, cb_s = _sample_layer("b", i, y_s, cache_k, cache_v, state_b, j, w_b, ntok,
                                      prev_state=cb_s)
            cb_p.append(st_p)
    kv_shape = (depth, n_prompt, N_MEM, HEADS, HEAD_DIM)
    return (y_p, y_s.reshape(n_sample, ntok, d), k_p.reshape(kv_shape), v_p.reshape(kv_shape),
            jnp.stack(ca_p), jnp.stack(cb_p), jnp.stack(ca_s),
            jnp.transpose(cb_s, (0, 2, 1, 3)))
```

```python
import jax
import jax.numpy as jnp
from jax import lax
from jax.experimental import pallas as pl
from jax.experimental.pallas import tpu as pltpu

F32 = jnp.float32
BF16 = jnp.bfloat16

D_MODEL = 1024
N_MEM = 256
HEADS = 4
HEAD_DIM = 128
XA = HEADS * HEAD_DIM
BRANCH = 2 * D_MODEL
MIX = BRANCH - XA
CONV_A = 3
CONV_B = 31
EPS = 1e-6
SCALE = HEAD_DIM ** -0.5

SUBLANES = 8
LANES = 128
CHUNK = 512
PROMPT_TILE = 512
SAMPLE_SEQS = SUBLANES
SAMPLE_TILE_SEQS = 2 * SAMPLE_SEQS
HDR_A = SUBLANES
HDR_B = 4 * SUBLANES
OFF_B = HDR_B - (CONV_B - 1)
CONV_ROWS = 64
ROW_PITCH = 2
VMEM_LIMIT = 58 * 1024 * 1024


def _dot(a, b):
    return jnp.dot(a, b, preferred_element_type=F32)


def _rms(x, g):
    ms = jnp.mean(x * x, axis=-1, keepdims=True)
    return x * lax.rsqrt(ms + EPS) * g


def _silu(z):
    return z * jax.nn.sigmoid(z)


def _layer_norm(c, g, b):
    mu = jnp.mean(c, axis=-1, keepdims=True)
    xc = c - mu
    var = jnp.mean(xc * xc, axis=-1, keepdims=True)
    return xc * lax.rsqrt(var + EPS) * g + b


def _prompt_attention(q, zx, k_ref, v_ref, br_ref):
    tm = q.shape[0]
    heads = [slice(h * HEAD_DIM, (h + 1) * HEAD_DIM) for h in range(HEADS)]
    s = jnp.concatenate(
        [lax.dot_general((q[:, hs] * SCALE).astype(BF16), k_ref[:, hs], (((1,), (1,)), ((), ())),
                         preferred_element_type=F32) for hs in heads], axis=0)
    m = jnp.max(s, axis=-1, keepdims=True)
    p = jnp.exp(s - m)
    inv_l = 1.0 / jnp.sum(p, axis=-1, keepdims=True)
    p = p.astype(BF16)
    for h, hs in enumerate(heads):
        rs = slice(h * tm, (h + 1) * tm)
        o = _dot(p[rs, :], v_ref[:, hs]) * inv_l[rs, :]
        br_ref[:, MIX + h * HEAD_DIM:MIX + (h + 1) * HEAD_DIM] = (
            o * _silu(zx[:, hs])).astype(BF16)


def _finish(x, br_ref, wout_ref, gpost_ref, y_ref):
    out = _dot(br_ref[...], wout_ref[...])
    y_ref[...] = x + _rms(out, gpost_ref[...])


def _head_rows(h):
    return pl.ds(h, N_MEM, stride=HEADS)


def _rows(start, n):
    return pl.ds(ROW_PITCH * start, n, stride=ROW_PITCH)


def _memkv_kernel(mem_ref, g_ref, w_ref, k_ref, v_ref, kb_ref, vb_ref):
    mn = _rms(mem_ref[...], g_ref[...]).astype(BF16)
    kv = _dot(mn, w_ref[...])
    k = kv[:, :XA]
    v = kv[:, XA:]
    for h in range(HEADS):
        hs = slice(h * HEAD_DIM, (h + 1) * HEAD_DIM)
        k_ref[_head_rows(h), :] = k[:, hs]
        v_ref[_head_rows(h), :] = v[:, hs]
    kb_ref[...] = k.astype(BF16)
    vb_ref[...] = v.astype(BF16)


def _memory_kv(mem, gains, w_kv):
    nb = mem.shape[0]
    nl = w_kv.shape[0]
    out_f = jax.ShapeDtypeStruct((nl, nb, N_MEM * HEADS, HEAD_DIM), F32)
    out_b = jax.ShapeDtypeStruct((nl, nb, N_MEM, XA), BF16)
    f_spec = pl.BlockSpec((None, None, N_MEM * HEADS, HEAD_DIM), lambda l, b: (l, b, 0, 0))
    b_spec = pl.BlockSpec((None, None, N_MEM, XA), lambda l, b: (l, b, 0, 0))
    return pl.pallas_call(
        _memkv_kernel,
        out_shape=(out_f, out_f, out_b, out_b),
        grid=(nl, nb),
        in_specs=[
            pl.BlockSpec((None, N_MEM, D_MODEL), lambda l, b: (b, 0, 0)),
            pl.BlockSpec((None, 1, D_MODEL), lambda l, b: (l, 0, 0)),
            pl.BlockSpec((None, D_MODEL, 2 * XA), lambda l, b: (l, 0, 0)),
        ],
        out_specs=(f_spec, f_spec, b_spec, b_spec),
        compiler_params=pltpu.CompilerParams(
            dimension_semantics=("arbitrary", "arbitrary")),
        name="memory_kv",
    )(mem, gains, w_kv)


def _prompt_a_kernel(x_ref, k_ref, v_ref, gpre_ref, gpost_ref, win_ref, cw_ref, wout_ref,
                     y_ref, st_ref, cbuf, br_ref):
    tm = x_ref.shape[0]
    nsl = CHUNK // LANES

    @pl.when(pl.program_id(1) == 0)
    def _():
        for sl in range(MIX // LANES):
            cbuf[sl, _rows(0, HDR_A), :] = jnp.zeros((HDR_A, LANES), F32)

    x = x_ref[...]
    xn = _rms(x, gpre_ref[...]).astype(BF16)

    def chunk(j):
        c0 = j * CHUNK
        hin = _dot(xn, win_ref[:, c0:c0 + CHUNK])
        cg = _dot(xn, win_ref[:, 2 * MIX + c0:2 * MIX + c0 + CHUNK])
        u = cg * hin
        bg = _dot(xn, win_ref[:, MIX + c0:MIX + c0 + CHUNK])
        z = _dot(xn, win_ref[:, 3 * MIX + c0:3 * MIX + c0 + CHUNK])
        gate = bg * _silu(z)
        st_ref[:, c0:c0 + CHUNK] = u[tm - (CONV_A - 1):, :]
        for lb in range(nsl):
            sl = j * nsl + lb
            ls = slice(c0 + lb * LANES, c0 + (lb + 1) * LANES)
            ul = u[:, lb * LANES:(lb + 1) * LANES]
            cbuf[sl, _rows(HDR_A, tm), :] = ul
            conv = (cw_ref[2:3, ls] * ul
                    + cw_ref[1:2, ls] * cbuf[sl, _rows(HDR_A - 1, tm), :]
                    + cw_ref[0:1, ls] * cbuf[sl, _rows(HDR_A - 2, tm), :])
            br_ref[:, ls] = (gate[:, lb * LANES:(lb + 1) * LANES] * conv).astype(BF16)
            cbuf[sl, _rows(0, HDR_A), :] = ul[tm - HDR_A:, :]

    chunk(0)
    zx = _dot(xn, win_ref[:, 4 * MIX:4 * MIX + XA])
    q = _dot(xn, win_ref[:, 3 * MIX + BRANCH:])
    _prompt_attention(q, zx, k_ref, v_ref, br_ref)
    chunk(1)
    chunk(2)
    _finish(x, br_ref, wout_ref, gpost_ref, y_ref)


def _prompt_b_kernel(x_ref, k_ref, v_ref, gpre_ref, gpost_ref, win_ref, cw_ref, cb_ref,
                     lng_ref, lnb_ref, wout_ref, y_ref, st_ref, cbuf, cscr, br_ref):
    tm = x_ref.shape[0]
    nsl = CHUNK // LANES

    @pl.when(pl.program_id(1) == 0)
    def _():
        for sl in range(MIX // LANES):
            cbuf[sl, _rows(0, HDR_B), :] = jnp.zeros((HDR_B, LANES), F32)

    x = x_ref[...]
    xn = _rms(x, gpre_ref[...]).astype(BF16)
    for j in range(MIX // CHUNK):
        c0 = j * CHUNK
        val = _dot(xn, win_ref[:, c0:c0 + CHUNK])
        glu = _dot(xn, win_ref[:, MIX + c0:MIX + c0 + CHUNK])
        u = val * jax.nn.sigmoid(glu)
        for lb in range(nsl):
            cbuf[j * nsl + lb, _rows(HDR_B, tm), :] = u[:, lb * LANES:(lb + 1) * LANES]
    for sl in range(MIX // LANES):
        ls = slice(sl * LANES, (sl + 1) * LANES)
        for rb in range(tm // CONV_ROWS):
            r0 = rb * CONV_ROWS + OFF_B
            acc = cb_ref[:, ls] + cw_ref[0:1, ls] * cbuf[sl, _rows(r0, CONV_ROWS), :]
            for k in range(1, CONV_B):
                acc = acc + cw_ref[k:k + 1, ls] * cbuf[sl, _rows(r0 + k, CONV_ROWS), :]
            cscr[rb * CONV_ROWS:(rb + 1) * CONV_ROWS, ls] = acc
        tail = cbuf[sl, _rows(tm, HDR_B), :]
        st_ref[:, ls] = tail[OFF_B:, :]
        cbuf[sl, _rows(0, HDR_B), :] = tail
    mixv = _silu(_layer_norm(cscr[...], lng_ref[...], lnb_ref[...]))
    for j in range(MIX // CHUNK):
        c0 = j * CHUNK
        z = _dot(xn, win_ref[:, 2 * MIX + c0:2 * MIX + c0 + CHUNK])
        br_ref[:, c0:c0 + CHUNK] = (mixv[:, c0:c0 + CHUNK] * _silu(z)).astype(BF16)
    zx = _dot(xn, win_ref[:, 3 * MIX:3 * MIX + XA])
    q = _dot(xn, win_ref[:, 2 * MIX + BRANCH:])
    _prompt_attention(q, zx, k_ref, v_ref, br_ref)
    _finish(x, br_ref, wout_ref, gpost_ref, y_ref)


def _resident(w, j):
    idx = (j,) + (0,) * (w.ndim - 1)
    return pl.BlockSpec((None,) + w.shape[1:], lambda *_: idx, pipeline_mode=pl.Buffered(1))


def _prompt_layer(kind, layer, j, x, kb, vb, weights):
    nb, t, d = x.shape
    tm = PROMPT_TILE
    wspecs = [_resident(w, j) for w in weights]
    if kind == "a":
        kern = _prompt_a_kernel
        nstate = CONV_A - 1
        scratch = [pltpu.VMEM((MIX // LANES, ROW_PITCH * (HDR_A + tm), LANES), F32),
                   pltpu.VMEM((tm, BRANCH), BF16)]
    else:
        kern = _prompt_b_kernel
        nstate = CONV_B - 1
        scratch = [pltpu.VMEM((MIX // LANES, ROW_PITCH * (HDR_B + tm), LANES), F32),
                   pltpu.VMEM((tm, MIX), F32), pltpu.VMEM((tm, BRANCH), BF16)]
    kv_spec = pl.BlockSpec((None, None, N_MEM, XA), lambda b, i: (layer, b, 0, 0))
    return pl.pallas_call(
        kern,
        out_shape=(jax.ShapeDtypeStruct((nb, t, d), F32),
                   jax.ShapeDtypeStruct((nb, nstate, MIX), F32)),
        grid=(nb, t // tm),
        in_specs=[pl.BlockSpec((None, tm, d), lambda b, i: (b, i, 0)), kv_spec, kv_spec] + wspecs,
        out_specs=(pl.BlockSpec((None, tm, d), lambda b, i: (b, i, 0)),
                   pl.BlockSpec((None, nstate, MIX), lambda b, i: (b, 0, 0))),
        scratch_shapes=scratch,
        compiler_params=pltpu.CompilerParams(
            dimension_semantics=("arbitrary", "arbitrary"),
            vmem_limit_bytes=VMEM_LIMIT),
        name="prompt_layer_" + kind,
    )(x, kb, vb, *weights)


def _sample_attention(q_ref, zx_ref, k_ref, v_ref, br_ref, r0, nseq, ntok):
    rows = HEADS * ntok
    row_head = lax.broadcasted_iota(jnp.int32, (rows, XA), 0) // ntok
    lane_head = lax.broadcasted_iota(jnp.int32, (rows, XA), 1) // HEAD_DIM
    diag = row_head == lane_head
    scores = []
    for g in range(nseq):
        kg = jnp.concatenate([k_ref[g, _head_rows(h), :] for h in range(HEADS)], axis=1)
        qg = q_ref[pl.ds(r0 + g * ntok, ntok), :]
        qd = jnp.where(diag, jnp.concatenate([qg] * HEADS, axis=0), 0.0).astype(BF16)
        scores.append(lax.dot_general(qd, kg.astype(BF16), (((1,), (1,)), ((), ())),
                                      preferred_element_type=F32))
    s = jnp.concatenate(scores, axis=0)
    m = jnp.max(s, axis=-1, keepdims=True)
    p = jnp.exp(s - m)
    inv_l = 1.0 / jnp.sum(p, axis=-1, keepdims=True)
    p = p.astype(BF16)
    outs = []
    for g in range(nseq):
        vg = jnp.concatenate([v_ref[g, _head_rows(h), :] for h in range(HEADS)], axis=1)
        gs = slice(g * rows, (g + 1) * rows)
        od = _dot(p[gs, :], vg.astype(BF16)) * inv_l[gs, :]
        outs.append(jnp.concatenate(
            [od[h * ntok:(h + 1) * ntok, h * HEAD_DIM:(h + 1) * HEAD_DIM] for h in range(HEADS)],
            axis=1))
    rs = pl.ds(r0, nseq * ntok)
    br_ref[rs, MIX:] = (jnp.concatenate(outs, axis=0) * zx_ref[rs, :]).astype(BF16)


def _sample_a_kernel(x_ref, k_ref, v_ref, st_ref, gpre_ref, gpost_ref, win_ref, cw_ref, wout_ref,
                     y_ref, nst_ref, cbuf, q_ref, zx_ref, br_ref):
    nseq = st_ref.shape[0]
    ntok = x_ref.shape[0] // nseq
    nsub = k_ref.shape[0]
    per = HDR_A + ntok
    j = pl.program_id(1)

    @pl.when(j == 0)
    def _():
        xn = _rms(x_ref[...], gpre_ref[...]).astype(BF16)
        for g in range(nseq):
            cbuf[g * per + HDR_A - (CONV_A - 1):g * per + HDR_A, :] = st_ref[g]
        for c in range(MIX // CHUNK):
            c0 = c * CHUNK
            cs = slice(c0, c0 + CHUNK)
            hin = _dot(xn, win_ref[:, c0:c0 + CHUNK])
            cg = _dot(xn, win_ref[:, 2 * MIX + c0:2 * MIX + c0 + CHUNK])
            u = cg * hin
            convs = []
            for g in range(nseq):
                b0 = g * per + HDR_A
                cbuf[b0:b0 + ntok, cs] = u[g * ntok:(g + 1) * ntok, :]
                convs.append(cw_ref[2:3, cs] * cbuf[b0:b0 + ntok, cs]
                             + cw_ref[1:2, cs] * cbuf[b0 - 1:b0 - 1 + ntok, cs]
                             + cw_ref[0:1, cs] * cbuf[b0 - 2:b0 - 2 + ntok, cs])
            conv = jnp.concatenate(convs, axis=0)
            bg = _dot(xn, win_ref[:, MIX + c0:MIX + c0 + CHUNK])
            z = _dot(xn, win_ref[:, 3 * MIX + c0:3 * MIX + c0 + CHUNK])
            br_ref[:, cs] = (bg * conv * _silu(z)).astype(BF16)
        for g in range(nseq):
            b0 = g * per + HDR_A
            nst_ref[g] = cbuf[b0 + ntok - (CONV_A - 1):b0 + ntok, :]
        zx_ref[...] = _silu(_dot(xn, win_ref[:, 4 * MIX:4 * MIX + XA]))
        q_ref[...] = _dot(xn, win_ref[:, 3 * MIX + BRANCH:]) * SCALE

    r0 = pl.multiple_of(j * (nsub * ntok), nsub * ntok)
    _sample_attention(q_ref, zx_ref, k_ref, v_ref, br_ref, r0, nsub, ntok)

    @pl.when(j == pl.num_programs(1) - 1)
    def _():
        _finish(x_ref[...], br_ref, wout_ref, gpost_ref, y_ref)


def _sample_b_kernel(x_ref, k_ref, v_ref, st_ref, gpre_ref, gpost_ref, win_ref, cw_ref, cb_ref,
                     lng_ref, lnb_ref, wout_ref, *rest):
    y_ref, nst_ref, ubuf, cscr, tscr, q_ref, zx_ref, br_ref = rest[-8:]
    nstate = CONV_B - 1
    nseq = st_ref.shape[1]
    ntok = x_ref.shape[0] // nseq
    nsub = k_ref.shape[0]
    nlb = MIX // LANES
    nsl = CHUNK // LANES
    j = pl.program_id(1)

    @pl.when(j == 0)
    def _():
        xn = _rms(x_ref[...], gpre_ref[...]).astype(BF16)
        for c in range(MIX // CHUNK):
            c0 = c * CHUNK
            val = _dot(xn, win_ref[:, c0:c0 + CHUNK])
            glu = _dot(xn, win_ref[:, MIX + c0:MIX + c0 + CHUNK])
            u = val * jax.nn.sigmoid(glu)
            for lb in range(nsl):
                tscr[c * nsl + lb] = u[:, lb * LANES:(lb + 1) * LANES]
        for t in range(ntok):
            for lb in range(nlb):
                ubuf[t * nseq:(t + 1) * nseq, lb * LANES:(lb + 1) * LANES] = (
                    tscr[lb, pl.ds(t, nseq, stride=ntok), :])

        def full(i, cs):
            if i < nstate:
                return st_ref[i, :, cs]
            return ubuf[(i - nstate) * nseq:(i - nstate + 1) * nseq, cs]

        for i in range(nstate):
            nst_ref[i] = full(ntok + i, slice(None))
        for t in range(ntok):
            for c in range(MIX // CHUNK):
                cs = slice(c * CHUNK, (c + 1) * CHUNK)
                acc = cb_ref[:, cs] + cw_ref[0:1, cs] * full(t, cs)
                for k in range(1, CONV_B):
                    acc = acc + cw_ref[k:k + 1, cs] * full(t + k, cs)
                cscr[t * nseq:(t + 1) * nseq, cs] = acc
        for t in range(ntok):
            mix_t = _silu(_layer_norm(cscr[t * nseq:(t + 1) * nseq, :], lng_ref[...], lnb_ref[...]))
            for lb in range(nlb):
                tscr[lb, pl.ds(t, nseq, stride=ntok), :] = mix_t[:, lb * LANES:(lb + 1) * LANES]
        for c in range(MIX // CHUNK):
            c0 = c * CHUNK
            z = _dot(xn, win_ref[:, 2 * MIX + c0:2 * MIX + c0 + CHUNK])
            mixv = jnp.concatenate([tscr[c * nsl + lb] for lb in range(nsl)], axis=1)
            br_ref[:, c0:c0 + CHUNK] = (mixv * _silu(z)).astype(BF16)
        zx_ref[...] = _silu(_dot(xn, win_ref[:, 3 * MIX:3 * MIX + XA]))
        q_ref[...] = _dot(xn, win_ref[:, 2 * MIX + BRANCH:]) * SCALE

    r0 = pl.multiple_of(j * (nsub * ntok), nsub * ntok)
    _sample_attention(q_ref, zx_ref, k_ref, v_ref, br_ref, r0, nsub, ntok)

    @pl.when(j == pl.num_programs(1) - 1)
    def _():
        _finish(x_ref[...], br_ref, wout_ref, gpost_ref, y_ref)


def _sample_layer(kind, layer, x, cache_k, cache_v, state, sidx, weights, ntok, prev_state=None):
    rows, d = x.shape
    nseq = SAMPLE_TILE_SEQS
    nsub = SAMPLE_SEQS
    tm = nseq * ntok
    nj = nseq // nsub
    nseq_total = rows // ntok
    wspecs = [_resident(w, sidx) for w in weights]
    kv_spec = pl.BlockSpec((None, nsub, N_MEM * HEADS, HEAD_DIM),
                           lambda i, j: (layer, i * nj + j, 0, 0))
    x_spec = pl.BlockSpec((tm, d), lambda i, j: (i, 0))
    attn_scratch = [pltpu.VMEM((tm, XA), F32), pltpu.VMEM((tm, XA), F32),
                    pltpu.VMEM((tm, BRANCH), BF16)]
    operands = [x, cache_k, cache_v, state]
    aliases = {}
    extra_specs = []
    if kind == "a":
        kern = _sample_a_kernel
        nstate = CONV_A - 1
        st_in = pl.BlockSpec((None, nseq, nstate, MIX), lambda i, j: (sidx, i, 0, 0))
        st_shape = jax.ShapeDtypeStruct((nseq_total, nstate, MIX), F32)
        st_out = pl.BlockSpec((nseq, nstate, MIX), lambda i, j: (i, 0, 0))
        scratch = [pltpu.VMEM((nseq * (HDR_A + ntok), MIX), F32)] + attn_scratch
    else:
        kern = _sample_b_kernel
        nstate = CONV_B - 1
        st_in = pl.BlockSpec((None, nstate, nseq, MIX), lambda i, j: (sidx, 0, i, 0),
                             pipeline_mode=pl.Buffered(1))
        st_shape = jax.ShapeDtypeStruct(state.shape, F32)
        st_out = pl.BlockSpec((None, nstate, nseq, MIX), lambda i, j: (sidx, 0, i, 0))
        scratch = [pltpu.VMEM((tm, MIX), F32),
                   pltpu.VMEM((tm, MIX), F32),
                   pltpu.VMEM((MIX // LANES, tm, LANES), F32)] + attn_scratch
        if prev_state is not None:
            extra_specs = [pl.BlockSpec(memory_space=pl.ANY)]
            aliases = {4 + len(weights): 1}
    in_specs = [x_spec, kv_spec, kv_spec, st_in] + wspecs + extra_specs
    operands = operands + list(weights) + ([prev_state] if extra_specs else [])
    return pl.pallas_call(
        kern,
        out_shape=(jax.ShapeDtypeStruct((rows, d), F32), st_shape),
        grid=(rows // tm, nj),
        in_specs=in_specs,
        out_specs=(x_spec, st_out),
        scratch_shapes=scratch,
        input_output_aliases=aliases,
        compiler_params=pltpu.CompilerParams(
            dimension_semantics=("arbitrary", "arbitrary"),
            vmem_limit_bytes=VMEM_LIMIT),
        name="sample_layer_" + kind,
    )(*operands)


def kernel(x_prompt, x_sample, mem_prompt, cache_mem_k, cache_mem_v, state_conv_a, state_conv_b, a_norm_pre, a_norm_post, a_mem_norm, a_w_in, a_conv_w, a_w_kv, a_w_out, b_norm_pre, b_norm_post, b_mem_norm, b_w_in, b_conv_w, b_conv_b, b_ln_g, b_ln_b, b_w_kv, b_w_out):
    depth = cache_mem_k.shape[0]
    n_prompt = x_prompt.shape[0]
    n_sample, ntok, d = x_sample.shape

    def rows3(v):
        return v.reshape(v.shape[0], 1, v.shape[1])

    w_a = (rows3(a_norm_pre), rows3(a_norm_post), a_w_in.astype(BF16), a_conv_w,
           a_w_out.astype(BF16))
    w_b = (rows3(b_norm_pre), rows3(b_norm_post), b_w_in.astype(BF16), b_conv_w,
           rows3(b_conv_b), rows3(b_ln_g), rows3(b_ln_b), b_w_out.astype(BF16))

    w_kv = jnp.stack([(a_w_kv if i % 2 == 0 else b_w_kv)[i // 2] for i in range(depth)]).astype(BF16)
    g_mem = jnp.stack([(a_mem_norm if i % 2 == 0 else b_mem_norm)[i // 2] for i in range(depth)])
    k_p, v_p, kb_p, vb_p = _memory_kv(mem_prompt, g_mem.reshape(depth, 1, d), w_kv)

    cache_k = cache_mem_k.reshape(depth, n_sample, N_MEM * HEADS, HEAD_DIM)
    cache_v = cache_mem_v.reshape(depth, n_sample, N_MEM * HEADS, HEAD_DIM)
    state_b = jnp.transpose(state_conv_b, (0, 2, 1, 3))
    y_p = x_prompt
    y_s = x_sample.reshape(n_sample * ntok, d)
    ca_p, cb_p, ca_s = [], [], []
    cb_s = None
    for i in range(depth):
        j = i // 2
        if i % 2 == 0:
            y_p, st_p = _prompt_layer("a", i, j, y_p, kb_p, vb_p, w_a)
            y_s, st_s = _sample_layer("a", i, y_s, cache_k, cache_v, state_conv_a, j, w_a, ntok)
            ca_p.append(st_p)
            ca_s.append(st_s)
        else:
            y_p, st_p = _prompt_layer("b", i, j, y_p, kb_p, vb_p, w_b)
            y_s, cb_s = _sample_layer("b", i, y_s, cache_k, cache_v, state_b, j, w_b, ntok,
                                      prev_state=cb_s)
            cb_p.append(st_p)
    kv_shape = (depth, n_prompt, N_MEM, HEADS, HEAD_DIM)
    return (y_p, y_s.reshape(n_sample, ntok, d), k_p.reshape(kv_shape), v_p.reshape(kv_shape),
            jnp.stack(ca_p), jnp.stack(cb_p), jnp.stack(ca_s),
            jnp.transpose(cb_s, (0, 2, 1, 3)))
```

```python
import jax
import jax.numpy as jnp
from jax import lax
from jax.experimental import pallas as pl
from jax.experimental.pallas import tpu as pltpu

F32 = jnp.float32
BF16 = jnp.bfloat16

D_MODEL = 1024
N_MEM = 256
HEADS = 4
HEAD_DIM = 128
XA = HEADS * HEAD_DIM
BRANCH = 2 * D_MODEL
MIX = BRANCH - XA
CONV_A = 3
CONV_B = 31
EPS = 1e-6
SCALE = HEAD_DIM ** -0.5

SUBLANES = 8
LANES = 128
CHUNK = 512
PROMPT_TILE = 512
SAMPLE_SEQS = SUBLANES
SAMPLE_TILE_SEQS = 2 * SAMPLE_SEQS
KV_STREAMS = 4
MEMKV_SEQS = 4
HDR_A = SUBLANES
HDR_B = 4 * SUBLANES
OFF_B = HDR_B - (CONV_B - 1)
CONV_ROWS = 64
ROW_PITCH = 2
VMEM_LIMIT = 58 * 1024 * 1024


def _dot(a, b):
    return jnp.dot(a, b, preferred_element_type=F32)


def _rms(x, g):
    ms = jnp.mean(x * x, axis=-1, keepdims=True)
    return x * lax.rsqrt(ms + EPS) * g


def _silu(z):
    return z * jax.nn.sigmoid(z)


def _layer_norm(c, g, b):
    mu = jnp.mean(c, axis=-1, keepdims=True)
    xc = c - mu
    var = jnp.mean(xc * xc, axis=-1, keepdims=True)
    return xc * lax.rsqrt(var + EPS) * g + b


def _prompt_attention(q, zx, k_ref, v_ref, br_ref):
    tm = q.shape[0]
    heads = [slice(h * HEAD_DIM, (h + 1) * HEAD_DIM) for h in range(HEADS)]
    s = jnp.concatenate(
        [lax.dot_general((q[:, hs] * SCALE).astype(BF16), k_ref[:, hs], (((1,), (1,)), ((), ())),
                         preferred_element_type=F32) for hs in heads], axis=0)
    m = jnp.max(s, axis=-1, keepdims=True)
    p = jnp.exp(s - m)
    inv_l = 1.0 / jnp.sum(p, axis=-1, keepdims=True)
    p = p.astype(BF16)
    for h, hs in enumerate(heads):
        rs = slice(h * tm, (h + 1) * tm)
        o = _dot(p[rs, :], v_ref[:, hs]) * inv_l[rs, :]
        br_ref[:, MIX + h * HEAD_DIM:MIX + (h + 1) * HEAD_DIM] = (
            o * _silu(zx[:, hs])).astype(BF16)


def _finish(x, br_ref, wout_ref, gpost_ref, y_ref):
    out = _dot(br_ref[...], wout_ref[...])
    y_ref[...] = x + _rms(out, gpost_ref[...])


def _head_rows(h):
    return pl.ds(h, N_MEM, stride=HEADS)


def _rows(start, n):
    return pl.ds(ROW_PITCH * start, n, stride=ROW_PITCH)


def _memkv_kernel(mem_ref, g_ref, w_ref, k_ref, v_ref, kb_ref, vb_ref):
    nseq = k_ref.shape[0]
    mn = _rms(mem_ref[...], g_ref[...]).astype(BF16)
    kv = _dot(mn, w_ref[...])
    for b in range(nseq):
        k = kv[b * N_MEM:(b + 1) * N_MEM, :XA]
        v = kv[b * N_MEM:(b + 1) * N_MEM, XA:]
        for h in range(HEADS):
            hs = slice(h * HEAD_DIM, (h + 1) * HEAD_DIM)
            k_ref[b, _head_rows(h), :] = k[:, hs]
            v_ref[b, _head_rows(h), :] = v[:, hs]
        kb_ref[b] = k.astype(BF16)
        vb_ref[b] = v.astype(BF16)


def _memory_kv(mem, gains, w_kv):
    nb = mem.shape[0]
    nl = w_kv.shape[0]
    ns = MEMKV_SEQS
    out_f = jax.ShapeDtypeStruct((nl, nb, N_MEM * HEADS, HEAD_DIM), F32)
    out_b = jax.ShapeDtypeStruct((nl, nb, N_MEM, XA), BF16)
    f_spec = pl.BlockSpec((None, ns, N_MEM * HEADS, HEAD_DIM), lambda l, b: (l, b, 0, 0))
    b_spec = pl.BlockSpec((None, ns, N_MEM, XA), lambda l, b: (l, b, 0, 0))
    return pl.pallas_call(
        _memkv_kernel,
        out_shape=(out_f, out_f, out_b, out_b),
        grid=(nl, nb // ns),
        in_specs=[
            pl.BlockSpec((ns * N_MEM, D_MODEL), lambda l, b: (b, 0)),
            pl.BlockSpec((None, 1, D_MODEL), lambda l, b: (l, 0, 0)),
            pl.BlockSpec((None, D_MODEL, 2 * XA), lambda l, b: (l, 0, 0)),
        ],
        out_specs=(f_spec, f_spec, b_spec, b_spec),
        compiler_params=pltpu.CompilerParams(
            dimension_semantics=("arbitrary", "arbitrary")),
        name="memory_kv",
    )(mem.reshape(nb * N_MEM, D_MODEL), gains, w_kv)


def _prompt_a_kernel(x_ref, k_ref, v_ref, gpre_ref, gpost_ref, win_ref, cw_ref, wout_ref,
                     y_ref, st_ref, cbuf, br_ref):
    tm = x_ref.shape[0]
    nsl = CHUNK // LANES

    @pl.when(pl.program_id(1) == 0)
    def _():
        for sl in range(MIX // LANES):
            cbuf[sl, _rows(0, HDR_A), :] = jnp.zeros((HDR_A, LANES), F32)

    x = x_ref[...]
    xn = _rms(x, gpre_ref[...]).astype(BF16)

    def chunk(j):
        c0 = j * CHUNK
        hin = _dot(xn, win_ref[:, c0:c0 + CHUNK])
        cg = _dot(xn, win_ref[:, 2 * MIX + c0:2 * MIX + c0 + CHUNK])
        u = cg * hin
        bg = _dot(xn, win_ref[:, MIX + c0:MIX + c0 + CHUNK])
        z = _dot(xn, win_ref[:, 3 * MIX + c0:3 * MIX + c0 + CHUNK])
        gate = bg * _silu(z)
        st_ref[:, c0:c0 + CHUNK] = u[tm - (CONV_A - 1):, :]
        for lb in range(nsl):
            sl = j * nsl + lb
            ls = slice(c0 + lb * LANES, c0 + (lb + 1) * LANES)
            ul = u[:, lb * LANES:(lb + 1) * LANES]
            cbuf[sl, _rows(HDR_A, tm), :] = ul
            conv = (cw_ref[2:3, ls] * ul
                    + cw_ref[1:2, ls] * cbuf[sl, _rows(HDR_A - 1, tm), :]
                    + cw_ref[0:1, ls] * cbuf[sl, _rows(HDR_A - 2, tm), :])
            br_ref[:, ls] = (gate[:, lb * LANES:(lb + 1) * LANES] * conv).astype(BF16)
            cbuf[sl, _rows(0, HDR_A), :] = ul[tm - HDR_A:, :]

    chunk(0)
    zx = _dot(xn, win_ref[:, 4 * MIX:4 * MIX + XA])
    q = _dot(xn, win_ref[:, 3 * MIX + BRANCH:])
    _prompt_attention(q, zx, k_ref, v_ref, br_ref)
    chunk(1)
    chunk(2)
    _finish(x, br_ref, wout_ref, gpost_ref, y_ref)


def _prompt_b_kernel(x_ref, k_ref, v_ref, gpre_ref, gpost_ref, win_ref, cw_ref, cb_ref,
                     lng_ref, lnb_ref, wout_ref, y_ref, st_ref, cbuf, cscr, br_ref):
    tm = x_ref.shape[0]
    nsl = CHUNK // LANES

    @pl.when(pl.program_id(1) == 0)
    def _():
        for sl in range(MIX // LANES):
            cbuf[sl, _rows(0, HDR_B), :] = jnp.zeros((HDR_B, LANES), F32)

    x = x_ref[...]
    xn = _rms(x, gpre_ref[...]).astype(BF16)
    for j in range(MIX // CHUNK):
        c0 = j * CHUNK
        val = _dot(xn, win_ref[:, c0:c0 + CHUNK])
        glu = _dot(xn, win_ref[:, MIX + c0:MIX + c0 + CHUNK])
        u = val * jax.nn.sigmoid(glu)
        for lb in range(nsl):
            cbuf[j * nsl + lb, _rows(HDR_B, tm), :] = u[:, lb * LANES:(lb + 1) * LANES]
    for sl in range(MIX // LANES):
        ls = slice(sl * LANES, (sl + 1) * LANES)
        for rb in range(tm // CONV_ROWS):
            r0 = rb * CONV_ROWS + OFF_B
            acc = cb_ref[:, ls] + cw_ref[0:1, ls] * cbuf[sl, _rows(r0, CONV_ROWS), :]
            for k in range(1, CONV_B):
                acc = acc + cw_ref[k:k + 1, ls] * cbuf[sl, _rows(r0 + k, CONV_ROWS), :]
            cscr[rb * CONV_ROWS:(rb + 1) * CONV_ROWS, ls] = acc
        tail = cbuf[sl, _rows(tm, HDR_B), :]
        st_ref[:, ls] = tail[OFF_B:, :]
        cbuf[sl, _rows(0, HDR_B), :] = tail
    mixv = _silu(_layer_norm(cscr[...], lng_ref[...], lnb_ref[...]))
    for j in range(MIX // CHUNK):
        c0 = j * CHUNK
        z = _dot(xn, win_ref[:, 2 * MIX + c0:2 * MIX + c0 + CHUNK])
        br_ref[:, c0:c0 + CHUNK] = (mixv[:, c0:c0 + CHUNK] * _silu(z)).astype(BF16)
    zx = _dot(xn, win_ref[:, 3 * MIX:3 * MIX + XA])
    q = _dot(xn, win_ref[:, 2 * MIX + BRANCH:])
    _prompt_attention(q, zx, k_ref, v_ref, br_ref)
    _finish(x, br_ref, wout_ref, gpost_ref, y_ref)


def _resident(w, j):
    idx = (j,) + (0,) * (w.ndim - 1)
    return pl.BlockSpec((None,) + w.shape[1:], lambda *_: idx, pipeline_mode=pl.Buffered(1))


def _prompt_layer(kind, layer, j, x, kb, vb, weights):
    nb, t, d = x.shape
    tm = PROMPT_TILE
    wspecs = [_resident(w, j) for w in weights]
    if kind == "a":
        kern = _prompt_a_kernel
        nstate = CONV_A - 1
        scratch = [pltpu.VMEM((MIX // LANES, ROW_PITCH * (HDR_A + tm), LANES), F32),
                   pltpu.VMEM((tm, BRANCH), BF16)]
    else:
        kern = _prompt_b_kernel
        nstate = CONV_B - 1
        scratch = [pltpu.VMEM((MIX // LANES, ROW_PITCH * (HDR_B + tm), LANES), F32),
                   pltpu.VMEM((tm, MIX), F32), pltpu.VMEM((tm, BRANCH), BF16)]
    kv_spec = pl.BlockSpec((None, None, N_MEM, XA), lambda b, i: (layer, b, 0, 0))
    return pl.pallas_call(
        kern,
        out_shape=(jax.ShapeDtypeStruct((nb, t, d), F32),
                   jax.ShapeDtypeStruct((nb, nstate, MIX), F32)),
        grid=(nb, t // tm),
        in_specs=[pl.BlockSpec((None, tm, d), lambda b, i: (b, i, 0)), kv_spec, kv_spec] + wspecs,
        out_specs=(pl.BlockSpec((None, tm, d), lambda b, i: (b, i, 0)),
                   pl.BlockSpec((None, nstate, MIX), lambda b, i: (b, 0, 0))),
        scratch_shapes=scratch,
        compiler_params=pltpu.CompilerParams(
            dimension_semantics=("arbitrary", "arbitrary"),
            vmem_limit_bytes=VMEM_LIMIT),
        name="prompt_layer_" + kind,
    )(x, kb, vb, *weights)


def _sample_attention(q_ref, zx_ref, k_refs, v_refs, br_ref, r0, ntok):
    per = k_refs[0].shape[0]
    nseq = per * len(k_refs)
    rows = HEADS * ntok
    row_head = lax.broadcasted_iota(jnp.int32, (rows, XA), 0) // ntok
    lane_head = lax.broadcasted_iota(jnp.int32, (rows, XA), 1) // HEAD_DIM
    diag = row_head == lane_head
    scores = []
    for g in range(nseq):
        kg = jnp.concatenate(
            [k_refs[g // per][g % per, _head_rows(h), :] for h in range(HEADS)], axis=1)
        qg = q_ref[pl.ds(r0 + g * ntok, ntok), :]
        qd = jnp.where(diag, jnp.concatenate([qg] * HEADS, axis=0), 0.0).astype(BF16)
        scores.append(lax.dot_general(qd, kg.astype(BF16), (((1,), (1,)), ((), ())),
                                      preferred_element_type=F32))
    s = jnp.concatenate(scores, axis=0)
    m = jnp.max(s, axis=-1, keepdims=True)
    p = jnp.exp(s - m)
    inv_l = 1.0 / jnp.sum(p, axis=-1, keepdims=True)
    p = p.astype(BF16)
    outs = []
    for g in range(nseq):
        vg = jnp.concatenate(
            [v_refs[g // per][g % per, _head_rows(h), :] for h in range(HEADS)], axis=1)
        gs = slice(g * rows, (g + 1) * rows)
        od = _dot(p[gs, :], vg.astype(BF16)) * inv_l[gs, :]
        outs.append(jnp.concatenate(
            [od[h * ntok:(h + 1) * ntok, h * HEAD_DIM:(h + 1) * HEAD_DIM] for h in range(HEADS)],
            axis=1))
    rs = pl.ds(r0, nseq * ntok)
    br_ref[rs, MIX:] = (jnp.concatenate(outs, axis=0) * zx_ref[rs, :]).astype(BF16)


def _sample_a_kernel(x_ref, k_refs, v_refs, st_ref, gpre_ref, gpost_ref, win_ref, cw_ref, wout_ref,
                     y_ref, nst_ref, cbuf, q_ref, zx_ref, br_ref):
    nseq = st_ref.shape[0]
    ntok = x_ref.shape[0] // nseq
    nsub = k_refs[0].shape[0] * len(k_refs)
    per = HDR_A + ntok
    j = pl.program_id(1)

    @pl.when(j == 0)
    def _():
        xn = _rms(x_ref[...], gpre_ref[...]).astype(BF16)
        for g in range(nseq):
            cbuf[g * per + HDR_A - (CONV_A - 1):g * per + HDR_A, :] = st_ref[g]
        for c in range(MIX // CHUNK):
            c0 = c * CHUNK
            cs = slice(c0, c0 + CHUNK)
            hin = _dot(xn, win_ref[:, c0:c0 + CHUNK])
            cg = _dot(xn, win_ref[:, 2 * MIX + c0:2 * MIX + c0 + CHUNK])
            u = cg * hin
            convs = []
            for g in range(nseq):
                b0 = g * per + HDR_A
                cbuf[b0:b0 + ntok, cs] = u[g * ntok:(g + 1) * ntok, :]
                convs.append(cw_ref[2:3, cs] * cbuf[b0:b0 + ntok, cs]
                             + cw_ref[1:2, cs] * cbuf[b0 - 1:b0 - 1 + ntok, cs]
                             + cw_ref[0:1, cs] * cbuf[b0 - 2:b0 - 2 + ntok, cs])
            conv = jnp.concatenate(convs, axis=0)
            bg = _dot(xn, win_ref[:, MIX + c0:MIX + c0 + CHUNK])
            z = _dot(xn, win_ref[:, 3 * MIX + c0:3 * MIX + c0 + CHUNK])
            br_ref[:, cs] = (bg * conv * _silu(z)).astype(BF16)
        for g in range(nseq):
            b0 = g * per + HDR_A
            nst_ref[g] = cbuf[b0 + ntok - (CONV_A - 1):b0 + ntok, :]
        zx_ref[...] = _silu(_dot(xn, win_ref[:, 4 * MIX:4 * MIX + XA]))
        q_ref[...] = _dot(xn, win_ref[:, 3 * MIX + BRANCH:]) * SCALE

    r0 = pl.multiple_of(j * (nsub * ntok), nsub * ntok)
    _sample_attention(q_ref, zx_ref, k_refs, v_refs, br_ref, r0, ntok)

    @pl.when(j == pl.num_programs(1) - 1)
    def _():
        _finish(x_ref[...], br_ref, wout_ref, gpost_ref, y_ref)


def _sample_b_kernel(x_ref, k_refs, v_refs, st_ref, gpre_ref, gpost_ref, win_ref, cw_ref, cb_ref,
                     lng_ref, lnb_ref, wout_ref, *rest):
    y_ref, nst_ref, ubuf, cscr, tscr, q_ref, zx_ref, br_ref = rest[-8:]
    nstate = CONV_B - 1
    nseq = st_ref.shape[1]
    ntok = x_ref.shape[0] // nseq
    nsub = k_refs[0].shape[0] * len(k_refs)
    nlb = MIX // LANES
    nsl = CHUNK // LANES
    j = pl.program_id(1)

    @pl.when(j == 0)
    def _():
        xn = _rms(x_ref[...], gpre_ref[...]).astype(BF16)
        for c in range(MIX // CHUNK):
            c0 = c * CHUNK
            val = _dot(xn, win_ref[:, c0:c0 + CHUNK])
            glu = _dot(xn, win_ref[:, MIX + c0:MIX + c0 + CHUNK])
            u = val * jax.nn.sigmoid(glu)
            for lb in range(nsl):
                tscr[c * nsl + lb] = u[:, lb * LANES:(lb + 1) * LANES]
        for t in range(ntok):
            for lb in range(nlb):
                ubuf[t * nseq:(t + 1) * nseq, lb * LANES:(lb + 1) * LANES] = (
                    tscr[lb, pl.ds(t, nseq, stride=ntok), :])

        def full(i, cs):
            if i < nstate:
                return st_ref[i, :, cs]
            return ubuf[(i - nstate) * nseq:(i - nstate + 1) * nseq, cs]

        for i in range(nstate):
            nst_ref[i] = full(ntok + i, slice(None))
        for t in range(ntok):
            for c in range(MIX // CHUNK):
                cs = slice(c * CHUNK, (c + 1) * CHUNK)
                acc = cb_ref[:, cs] + cw_ref[0:1, cs] * full(t, cs)
                for k in range(1, CONV_B):
                    acc = acc + cw_ref[k:k + 1, cs] * full(t + k, cs)
                cscr[t * nseq:(t + 1) * nseq, cs] = acc
        for t in range(ntok):
            mix_t = _silu(_layer_norm(cscr[t * nseq:(t + 1) * nseq, :], lng_ref[...], lnb_ref[...]))
            for lb in range(nlb):
                tscr[lb, pl.ds(t, nseq, stride=ntok), :] = mix_t[:, lb * LANES:(lb + 1) * LANES]
        for c in range(MIX // CHUNK):
            c0 = c * CHUNK
            z = _dot(xn, win_ref[:, 2 * MIX + c0:2 * MIX + c0 + CHUNK])
            mixv = jnp.concatenate([tscr[c * nsl + lb] for lb in range(nsl)], axis=1)
            br_ref[:, c0:c0 + CHUNK] = (mixv * _silu(z)).astype(BF16)
        zx_ref[...] = _silu(_dot(xn, win_ref[:, 3 * MIX:3 * MIX + XA]))
        q_ref[...] = _dot(xn, win_ref[:, 2 * MIX + BRANCH:]) * SCALE

    r0 = pl.multiple_of(j * (nsub * ntok), nsub * ntok)
    _sample_attention(q_ref, zx_ref, k_refs, v_refs, br_ref, r0, ntok)

    @pl.when(j == pl.num_programs(1) - 1)
    def _():
        _finish(x_ref[...], br_ref, wout_ref, gpost_ref, y_ref)


def _sample_layer(kind, layer, x, cache_k, cache_v, state, sidx, weights, ntok, prev_state=None):
    rows, d = x.shape
    nseq = SAMPLE_TILE_SEQS
    nsub = SAMPLE_SEQS
    tm = nseq * ntok
    nj = nseq // nsub
    nseq_total = rows // ntok
    wspecs = [_resident(w, sidx) for w in weights]
    per = nsub // KV_STREAMS
    kv_specs = [pl.BlockSpec((None, per, N_MEM * HEADS, HEAD_DIM),
                             lambda i, j, s=s: (layer, (i * nj + j) * KV_STREAMS + s, 0, 0))
                for s in range(KV_STREAMS)]
    x_spec = pl.BlockSpec((tm, d), lambda i, j: (i, 0))
    attn_scratch = [pltpu.VMEM((tm, XA), F32), pltpu.VMEM((tm, XA), F32),
                    pltpu.VMEM((tm, BRANCH), BF16)]
    operands = [x] + [cache_k] * KV_STREAMS + [cache_v] * KV_STREAMS + [state] + list(weights)
    aliases = {}
    extra_specs = []
    if kind == "a":
        kern = _sample_a_kernel
        nstate = CONV_A - 1
        st_in = pl.BlockSpec((None, nseq, nstate, MIX), lambda i, j: (sidx, i, 0, 0))
        st_shape = jax.ShapeDtypeStruct((nseq_total, nstate, MIX), F32)
        st_out = pl.BlockSpec((nseq, nstate, MIX), lambda i, j: (i, 0, 0))
        scratch = [pltpu.VMEM((nseq * (HDR_A + ntok), MIX), F32)] + attn_scratch
    else:
        kern = _sample_b_kernel
        nstate = CONV_B - 1
        st_in = pl.BlockSpec((None, nstate, nseq, MIX), lambda i, j: (sidx, 0, i, 0),
                             pipeline_mode=pl.Buffered(1))
        st_shape = jax.ShapeDtypeStruct(state.shape, F32)
        st_out = pl.BlockSpec((None, nstate, nseq, MIX), lambda i, j: (sidx, 0, i, 0))
        scratch = [pltpu.VMEM((tm, MIX), F32),
                   pltpu.VMEM((tm, MIX), F32),
                   pltpu.VMEM((MIX // LANES, tm, LANES), F32)] + attn_scratch
        extra_specs = [pl.BlockSpec(memory_space=pl.ANY)]
        aliases = {len(operands): 1}
        operands = operands + [prev_state]
    in_specs = [x_spec] + kv_specs + kv_specs + [st_in] + wspecs + extra_specs

    def body(x_ref, *refs):
        return kern(x_ref, refs[:KV_STREAMS], refs[KV_STREAMS:2 * KV_STREAMS],
                    *refs[2 * KV_STREAMS:])

    return pl.pallas_call(
        body,
        out_shape=(jax.ShapeDtypeStruct((rows, d), F32), st_shape),
        grid=(rows // tm, nj),
        in_specs=in_specs,
        out_specs=(x_spec, st_out),
        scratch_shapes=scratch,
        input_output_aliases=aliases,
        compiler_params=pltpu.CompilerParams(
            dimension_semantics=("arbitrary", "arbitrary"),
            vmem_limit_bytes=VMEM_LIMIT),
        name="sample_layer_" + kind,
    )(*operands)


def kernel(x_prompt, x_sample, mem_prompt, cache_mem_k, cache_mem_v, state_conv_a, state_conv_b, a_norm_pre, a_norm_post, a_mem_norm, a_w_in, a_conv_w, a_w_kv, a_w_out, b_norm_pre, b_norm_post, b_mem_norm, b_w_in, b_conv_w, b_conv_b, b_ln_g, b_ln_b, b_w_kv, b_w_out):
    depth = cache_mem_k.shape[0]
    n_prompt = x_prompt.shape[0]
    n_sample, ntok, d = x_sample.shape

    def rows3(v):
        return v.reshape(v.shape[0], 1, v.shape[1])

    w_a = (rows3(a_norm_pre), rows3(a_norm_post), a_w_in.astype(BF16), a_conv_w,
           a_w_out.astype(BF16))
    w_b = (rows3(b_norm_pre), rows3(b_norm_post), b_w_in.astype(BF16), b_conv_w,
           rows3(b_conv_b), rows3(b_ln_g), rows3(b_ln_b), b_w_out.astype(BF16))

    w_kv = jnp.stack([(a_w_kv if i % 2 == 0 else b_w_kv)[i // 2] for i in range(depth)]).astype(BF16)
    g_mem = jnp.stack([(a_mem_norm if i % 2 == 0 else b_mem_norm)[i // 2] for i in range(depth)])
    k_p, v_p, kb_p, vb_p = _memory_kv(mem_prompt, g_mem.reshape(depth, 1, d), w_kv)

    cache_k = cache_mem_k.reshape(depth, n_sample, N_MEM * HEADS, HEAD_DIM)
    cache_v = cache_mem_v.reshape(depth, n_sample, N_MEM * HEADS, HEAD_DIM)
    state_b = jnp.transpose(state_conv_b, (0, 2, 1, 3))
    y_p = x_prompt
    y_s = x_sample.reshape(n_sample * ntok, d)
    ca_p, cb_p, ca_s = [], [], []
    cb_s = jnp.zeros(state_b.shape, F32)
    for i in range(depth):
        j = i // 2
        if i % 2 == 0:
            y_p, st_p = _prompt_layer("a", i, j, y_p, kb_p, vb_p, w_a)
            y_s, st_s = _sample_layer("a", i, y_s, cache_k, cache_v, state_conv_a, j, w_a, ntok)
            ca_p.append(st_p)
            ca_s.append(st_s)
        else:
            y_p, st_p = _prompt_layer("b", i, j, y_p, kb_p, vb_p, w_b)
            y_s, cb_s = _sample_layer("b", i, y_s, cache_k, cache_v, state_b, j, w_b, ntok,
                                      prev_state=cb_s)
            cb_p.append(st_p)
    kv_shape = (depth, n_prompt, N_MEM, HEADS, HEAD_DIM)
    return (y_p, y_s.reshape(n_sample, ntok, d), k_p.reshape(kv_shape), v_p.reshape(kv_shape),
            jnp.stack(ca_p), jnp.stack(cb_p), jnp.stack(ca_s),
            jnp.transpose(cb_s, (0, 2, 1, 3)))
```

```python
import jax
import jax.numpy as jnp
from jax import lax
from jax.experimental import pallas as pl
from jax.experimental.pallas import tpu as pltpu

F32 = jnp.float32
BF16 = jnp.bfloat16

D_MODEL = 1024
N_MEM = 256
HEADS = 4
HEAD_DIM = 128
XA = HEADS * HEAD_DIM
BRANCH = 2 * D_MODEL
MIX = BRANCH - XA
CONV_A = 3
CONV_B = 31
EPS = 1e-6
SCALE = HEAD_DIM ** -0.5

SUBLANES = 8
LANES = 128
CHUNK = 512
PROMPT_TILE = 512
SAMPLE_TILE_SEQS = 2 * SUBLANES
SAMPLE_SEQS_A = SAMPLE_TILE_SEQS
SAMPLE_SEQS_B = SUBLANES
MEMKV_SEQS = 4
HDR_A = SUBLANES
HDR_B = 4 * SUBLANES
OFF_B = HDR_B - (CONV_B - 1)
CONV_ROWS = 64
ROW_PITCH = 2
VMEM_LIMIT = 58 * 1024 * 1024


def _dot(a, b):
    return jnp.dot(a, b, preferred_element_type=F32)


def _rms(x, g):
    ms = jnp.mean(x * x, axis=-1, keepdims=True)
    return x * lax.rsqrt(ms + EPS) * g


def _silu(z):
    return z * jax.nn.sigmoid(z)


def _layer_norm(c, g, b):
    mu = jnp.mean(c, axis=-1, keepdims=True)
    xc = c - mu
    var = jnp.mean(xc * xc, axis=-1, keepdims=True)
    return xc * lax.rsqrt(var + EPS) * g + b


def _prompt_attention(q, zx, k_ref, v_ref, br_ref):
    tm = q.shape[0]
    heads = [slice(h * HEAD_DIM, (h + 1) * HEAD_DIM) for h in range(HEADS)]
    s = jnp.concatenate(
        [lax.dot_general((q[:, hs] * SCALE).astype(BF16), k_ref[:, hs], (((1,), (1,)), ((), ())),
                         preferred_element_type=F32) for hs in heads], axis=0)
    m = jnp.max(s, axis=-1, keepdims=True)
    p = jnp.exp(s - m)
    inv_l = 1.0 / jnp.sum(p, axis=-1, keepdims=True)
    p = p.astype(BF16)
    for h, hs in enumerate(heads):
        rs = slice(h * tm, (h + 1) * tm)
        o = _dot(p[rs, :], v_ref[:, hs]) * inv_l[rs, :]
        br_ref[:, MIX + h * HEAD_DIM:MIX + (h + 1) * HEAD_DIM] = (
            o * _silu(zx[:, hs])).astype(BF16)


def _finish(x, br_ref, wout_ref, gpost_ref, y_ref):
    out = _dot(br_ref[...], wout_ref[...])
    y_ref[...] = x + _rms(out, gpost_ref[...])


def _head_rows(h):
    return pl.ds(h, N_MEM, stride=HEADS)


def _rows(start, n):
    return pl.ds(ROW_PITCH * start, n, stride=ROW_PITCH)


def _memkv_kernel(mem_ref, g_ref, w_ref, k_ref, v_ref, kb_ref, vb_ref):
    nseq = k_ref.shape[0]
    mn = _rms(mem_ref[...], g_ref[...]).astype(BF16)
    kv = _dot(mn, w_ref[...])
    for b in range(nseq):
        k = kv[b * N_MEM:(b + 1) * N_MEM, :XA]
        v = kv[b * N_MEM:(b + 1) * N_MEM, XA:]
        for h in range(HEADS):
            hs = slice(h * HEAD_DIM, (h + 1) * HEAD_DIM)
            k_ref[b, _head_rows(h), :] = k[:, hs]
            v_ref[b, _head_rows(h), :] = v[:, hs]
        kb_ref[b] = k.astype(BF16)
        vb_ref[b] = v.astype(BF16)


def _memory_kv(mem, gains, w_kv):
    nb = mem.shape[0]
    nl = w_kv.shape[0]
    ns = MEMKV_SEQS
    out_f = jax.ShapeDtypeStruct((nl, nb, N_MEM * HEADS, HEAD_DIM), F32)
    out_b = jax.ShapeDtypeStruct((nl, nb, N_MEM, XA), BF16)
    f_spec = pl.BlockSpec((None, ns, N_MEM * HEADS, HEAD_DIM), lambda l, b: (l, b, 0, 0))
    b_spec = pl.BlockSpec((None, ns, N_MEM, XA), lambda l, b: (l, b, 0, 0))
    return pl.pallas_call(
        _memkv_kernel,
        out_shape=(out_f, out_f, out_b, out_b),
        grid=(nl, nb // ns),
        in_specs=[
            pl.BlockSpec((ns * N_MEM, D_MODEL), lambda l, b: (b, 0)),
            pl.BlockSpec((None, 1, D_MODEL), lambda l, b: (l, 0, 0)),
            pl.BlockSpec((None, D_MODEL, 2 * XA), lambda l, b: (l, 0, 0)),
        ],
        out_specs=(f_spec, f_spec, b_spec, b_spec),
        compiler_params=pltpu.CompilerParams(
            dimension_semantics=("arbitrary", "arbitrary")),
        name="memory_kv",
    )(mem.reshape(nb * N_MEM, D_MODEL), gains, w_kv)


def _prompt_a_kernel(x_ref, k_ref, v_ref, gpre_ref, gpost_ref, win_ref, cw_ref, wout_ref,
                     y_ref, st_ref, cbuf, br_ref):
    tm = x_ref.shape[0]
    nsl = CHUNK // LANES

    @pl.when(pl.program_id(1) == 0)
    def _():
        for sl in range(MIX // LANES):
            cbuf[sl, _rows(0, HDR_A), :] = jnp.zeros((HDR_A, LANES), F32)

    x = x_ref[...]
    xn = _rms(x, gpre_ref[...]).astype(BF16)

    def chunk(j):
        c0 = j * CHUNK
        hin = _dot(xn, win_ref[:, c0:c0 + CHUNK])
        cg = _dot(xn, win_ref[:, 2 * MIX + c0:2 * MIX + c0 + CHUNK])
        u = cg * hin
        bg = _dot(xn, win_ref[:, MIX + c0:MIX + c0 + CHUNK])
        z = _dot(xn, win_ref[:, 3 * MIX + c0:3 * MIX + c0 + CHUNK])
        gate = bg * _silu(z)
        st_ref[:, c0:c0 + CHUNK] = u[tm - (CONV_A - 1):, :]
        for lb in range(nsl):
            sl = j * nsl + lb
            ls = slice(c0 + lb * LANES, c0 + (lb + 1) * LANES)
            ul = u[:, lb * LANES:(lb + 1) * LANES]
            cbuf[sl, _rows(HDR_A, tm), :] = ul
            conv = (cw_ref[2:3, ls] * ul
                    + cw_ref[1:2, ls] * cbuf[sl, _rows(HDR_A - 1, tm), :]
                    + cw_ref[0:1, ls] * cbuf[sl, _rows(HDR_A - 2, tm), :])
            br_ref[:, ls] = (gate[:, lb * LANES:(lb + 1) * LANES] * conv).astype(BF16)
            cbuf[sl, _rows(0, HDR_A), :] = ul[tm - HDR_A:, :]

    chunk(0)
    zx = _dot(xn, win_ref[:, 4 * MIX:4 * MIX + XA])
    q = _dot(xn, win_ref[:, 3 * MIX + BRANCH:])
    _prompt_attention(q, zx, k_ref, v_ref, br_ref)
    chunk(1)
    chunk(2)
    _finish(x, br_ref, wout_ref, gpost_ref, y_ref)


def _prompt_b_kernel(x_ref, k_ref, v_ref, gpre_ref, gpost_ref, win_ref, cw_ref, cb_ref,
                     lng_ref, lnb_ref, wout_ref, y_ref, st_ref, cbuf, cscr, br_ref):
    tm = x_ref.shape[0]
    nsl = CHUNK // LANES

    @pl.when(pl.program_id(1) == 0)
    def _():
        for sl in range(MIX // LANES):
            cbuf[sl, _rows(0, HDR_B), :] = jnp.zeros((HDR_B, LANES), F32)

    x = x_ref[...]
    xn = _rms(x, gpre_ref[...]).astype(BF16)
    for j in range(MIX // CHUNK):
        c0 = j * CHUNK
        val = _dot(xn, win_ref[:, c0:c0 + CHUNK])
        glu = _dot(xn, win_ref[:, MIX + c0:MIX + c0 + CHUNK])
        u = val * jax.nn.sigmoid(glu)
        for lb in range(nsl):
            cbuf[j * nsl + lb, _rows(HDR_B, tm), :] = u[:, lb * LANES:(lb + 1) * LANES]
    for sl in range(MIX // LANES):
        ls = slice(sl * LANES, (sl + 1) * LANES)
        for rb in range(tm // CONV_ROWS):
            r0 = rb * CONV_ROWS + OFF_B
            acc = cb_ref[:, ls] + cw_ref[0:1, ls] * cbuf[sl, _rows(r0, CONV_ROWS), :]
            for k in range(1, CONV_B):
                acc = acc + cw_ref[k:k + 1, ls] * cbuf[sl, _rows(r0 + k, CONV_ROWS), :]
            cscr[rb * CONV_ROWS:(rb + 1) * CONV_ROWS, ls] = acc
        tail = cbuf[sl, _rows(tm, HDR_B), :]
        st_ref[:, ls] = tail[OFF_B:, :]
        cbuf[sl, _rows(0, HDR_B), :] = tail
    mixv = _silu(_layer_norm(cscr[...], lng_ref[...], lnb_ref[...]))
    for j in range(MIX // CHUNK):
        c0 = j * CHUNK
        z = _dot(xn, win_ref[:, 2 * MIX + c0:2 * MIX + c0 + CHUNK])
        br_ref[:, c0:c0 + CHUNK] = (mixv[:, c0:c0 + CHUNK] * _silu(z)).astype(BF16)
    zx = _dot(xn, win_ref[:, 3 * MIX:3 * MIX + XA])
    q = _dot(xn, win_ref[:, 2 * MIX + BRANCH:])
    _prompt_attention(q, zx, k_ref, v_ref, br_ref)
    _finish(x, br_ref, wout_ref, gpost_ref, y_ref)


def _resident(w, j):
    idx = (j,) + (0,) * (w.ndim - 1)
    return pl.BlockSpec((None,) + w.shape[1:], lambda *_: idx, pipeline_mode=pl.Buffered(1))


def _prompt_layer(kind, layer, j, x, kb, vb, weights):
    nb, t, d = x.shape
    tm = PROMPT_TILE
    wspecs = [_resident(w, j) for w in weights]
    if kind == "a":
        kern = _prompt_a_kernel
        nstate = CONV_A - 1
        scratch = [pltpu.VMEM((MIX // LANES, ROW_PITCH * (HDR_A + tm), LANES), F32),
                   pltpu.VMEM((tm, BRANCH), BF16)]
    else:
        kern = _prompt_b_kernel
        nstate = CONV_B - 1
        scratch = [pltpu.VMEM((MIX // LANES, ROW_PITCH * (HDR_B + tm), LANES), F32),
                   pltpu.VMEM((tm, MIX), F32), pltpu.VMEM((tm, BRANCH), BF16)]
    kv_spec = pl.BlockSpec((None, None, N_MEM, XA), lambda b, i: (layer, b, 0, 0))
    return pl.pallas_call(
        kern,
        out_shape=(jax.ShapeDtypeStruct((nb, t, d), F32),
                   jax.ShapeDtypeStruct((nb, nstate, MIX), F32)),
        grid=(nb, t // tm),
        in_specs=[pl.BlockSpec((None, tm, d), lambda b, i: (b, i, 0)), kv_spec, kv_spec] + wspecs,
        out_specs=(pl.BlockSpec((None, tm, d), lambda b, i: (b, i, 0)),
                   pl.BlockSpec((None, nstate, MIX), lambda b, i: (b, 0, 0))),
        scratch_shapes=scratch,
        compiler_params=pltpu.CompilerParams(
            dimension_semantics=("arbitrary", "arbitrary"),
            vmem_limit_bytes=VMEM_LIMIT),
        name="prompt_layer_" + kind,
    )(x, kb, vb, *weights)


def _sample_attention(q_ref, zx_ref, k_ref, v_ref, br_ref, r0, ntok):
    nseq = k_ref.shape[0]
    rows = HEADS * ntok
    row_head = lax.broadcasted_iota(jnp.int32, (rows, XA), 0) // ntok
    lane_head = lax.broadcasted_iota(jnp.int32, (rows, XA), 1) // HEAD_DIM
    diag = row_head == lane_head
    scores = []
    for g in range(nseq):
        kg = jnp.concatenate([k_ref[g, _head_rows(h), :] for h in range(HEADS)], axis=1)
        qg = q_ref[pl.ds(r0 + g * ntok, ntok), :]
        qd = jnp.where(diag, jnp.concatenate([qg] * HEADS, axis=0), 0.0).astype(BF16)
        scores.append(lax.dot_general(qd, kg.astype(BF16), (((1,), (1,)), ((), ())),
                                      preferred_element_type=F32))
    s = jnp.concatenate(scores, axis=0)
    m = jnp.max(s, axis=-1, keepdims=True)
    p = jnp.exp(s - m)
    inv_l = 1.0 / jnp.sum(p, axis=-1, keepdims=True)
    p = p.astype(BF16)
    outs = []
    for g in range(nseq):
        vg = jnp.concatenate([v_ref[g, _head_rows(h), :] for h in range(HEADS)], axis=1)
        gs = slice(g * rows, (g + 1) * rows)
        od = _dot(p[gs, :], vg.astype(BF16)) * inv_l[gs, :]
        outs.append(jnp.concatenate(
            [od[h * ntok:(h + 1) * ntok, h * HEAD_DIM:(h + 1) * HEAD_DIM] for h in range(HEADS)],
            axis=1))
    rs = pl.ds(r0, nseq * ntok)
    br_ref[rs, MIX:] = (jnp.concatenate(outs, axis=0) * zx_ref[rs, :]).astype(BF16)


def _sample_a_kernel(x_ref, k_ref, v_ref, st_ref, gpre_ref, gpost_ref, win_ref, cw_ref, wout_ref,
                     y_ref, nst_ref, cbuf, q_ref, zx_ref, br_ref):
    nseq = st_ref.shape[0]
    ntok = x_ref.shape[0] // nseq
    nsub = k_ref.shape[0]
    per = HDR_A + ntok
    j = pl.program_id(1)

    @pl.when(j == 0)
    def _():
        xn = _rms(x_ref[...], gpre_ref[...]).astype(BF16)
        for g in range(nseq):
            cbuf[g * per + HDR_A - (CONV_A - 1):g * per + HDR_A, :] = st_ref[g]
        for c in range(MIX // CHUNK):
            c0 = c * CHUNK
            cs = slice(c0, c0 + CHUNK)
            hin = _dot(xn, win_ref[:, c0:c0 + CHUNK])
            cg = _dot(xn, win_ref[:, 2 * MIX + c0:2 * MIX + c0 + CHUNK])
            u = cg * hin
            convs = []
            for g in range(nseq):
                b0 = g * per + HDR_A
                cbuf[b0:b0 + ntok, cs] = u[g * ntok:(g + 1) * ntok, :]
                convs.append(cw_ref[2:3, cs] * cbuf[b0:b0 + ntok, cs]
                             + cw_ref[1:2, cs] * cbuf[b0 - 1:b0 - 1 + ntok, cs]
                             + cw_ref[0:1, cs] * cbuf[b0 - 2:b0 - 2 + ntok, cs])
            conv = jnp.concatenate(convs, axis=0)
            bg = _dot(xn, win_ref[:, MIX + c0:MIX + c0 + CHUNK])
            z = _dot(xn, win_ref[:, 3 * MIX + c0:3 * MIX + c0 + CHUNK])
            br_ref[:, cs] = (bg * conv * _silu(z)).astype(BF16)
        for g in range(nseq):
            b0 = g * per + HDR_A
            nst_ref[g] = cbuf[b0 + ntok - (CONV_A - 1):b0 + ntok, :]
        zx_ref[...] = _silu(_dot(xn, win_ref[:, 4 * MIX:4 * MIX + XA]))
        q_ref[...] = _dot(xn, win_ref[:, 3 * MIX + BRANCH:]) * SCALE

    r0 = pl.multiple_of(j * (nsub * ntok), nsub * ntok)
    _sample_attention(q_ref, zx_ref, k_ref, v_ref, br_ref, r0, ntok)

    @pl.when(j == pl.num_programs(1) - 1)
    def _():
        _finish(x_ref[...], br_ref, wout_ref, gpost_ref, y_ref)


def _sample_b_kernel(x_ref, k_ref, v_ref, st_ref, gpre_ref, gpost_ref, win_ref, cw_ref, cb_ref,
                     lng_ref, lnb_ref, wout_ref, *rest):
    y_ref, nst_ref, ubuf, cscr, tscr, q_ref, zx_ref, br_ref = rest[-8:]
    nstate = CONV_B - 1
    nseq = st_ref.shape[1]
    ntok = x_ref.shape[0] // nseq
    nsub = k_ref.shape[0]
    nlb = MIX // LANES
    nsl = CHUNK // LANES
    j = pl.program_id(1)

    @pl.when(j == 0)
    def _():
        xn = _rms(x_ref[...], gpre_ref[...]).astype(BF16)
        for c in range(MIX // CHUNK):
            c0 = c * CHUNK
            val = _dot(xn, win_ref[:, c0:c0 + CHUNK])
            glu = _dot(xn, win_ref[:, MIX + c0:MIX + c0 + CHUNK])
            u = val * jax.nn.sigmoid(glu)
            for lb in range(nsl):
                tscr[c * nsl + lb] = u[:, lb * LANES:(lb + 1) * LANES]
        for t in range(ntok):
            for lb in range(nlb):
                ubuf[t * nseq:(t + 1) * nseq, lb * LANES:(lb + 1) * LANES] = (
                    tscr[lb, pl.ds(t, nseq, stride=ntok), :])

        def full(i, cs):
            if i < nstate:
                return st_ref[i, :, cs]
            return ubuf[(i - nstate) * nseq:(i - nstate + 1) * nseq, cs]

        for i in range(nstate):
            nst_ref[i] = full(ntok + i, slice(None))
        for t in range(ntok):
            for c in range(MIX // CHUNK):
                cs = slice(c * CHUNK, (c + 1) * CHUNK)
                acc = cb_ref[:, cs] + cw_ref[0:1, cs] * full(t, cs)
                for k in range(1, CONV_B):
                    acc = acc + cw_ref[k:k + 1, cs] * full(t + k, cs)
                cscr[t * nseq:(t + 1) * nseq, cs] = acc
        for t in range(ntok):
            mix_t = _silu(_layer_norm(cscr[t * nseq:(t + 1) * nseq, :], lng_ref[...], lnb_ref[...]))
            for lb in range(nlb):
                tscr[lb, pl.ds(t, nseq, stride=ntok), :] = mix_t[:, lb * LANES:(lb + 1) * LANES]
        for c in range(MIX // CHUNK):
            c0 = c * CHUNK
            z = _dot(xn, win_ref[:, 2 * MIX + c0:2 * MIX + c0 + CHUNK])
            mixv = jnp.concatenate([tscr[c * nsl + lb] for lb in range(nsl)], axis=1)
            br_ref[:, c0:c0 + CHUNK] = (mixv * _silu(z)).astype(BF16)
        zx_ref[...] = _silu(_dot(xn, win_ref[:, 3 * MIX:3 * MIX + XA]))
        q_ref[...] = _dot(xn, win_ref[:, 2 * MIX + BRANCH:]) * SCALE

    r0 = pl.multiple_of(j * (nsub * ntok), nsub * ntok)
    _sample_attention(q_ref, zx_ref, k_ref, v_ref, br_ref, r0, ntok)

    @pl.when(j == pl.num_programs(1) - 1)
    def _():
        _finish(x_ref[...], br_ref, wout_ref, gpost_ref, y_ref)


def _sample_layer(kind, layer, x, cache_k, cache_v, state, sidx, weights, ntok, prev_state=None):
    rows, d = x.shape
    nseq = SAMPLE_TILE_SEQS
    nsub = SAMPLE_SEQS_A if kind == "a" else SAMPLE_SEQS_B
    tm = nseq * ntok
    nj = nseq // nsub
    nseq_total = rows // ntok
    wspecs = [_resident(w, sidx) for w in weights]
    kv_spec = pl.BlockSpec((None, nsub, N_MEM * HEADS, HEAD_DIM),
                           lambda i, j: (layer, i * nj + j, 0, 0))
    x_spec = pl.BlockSpec((tm, d), lambda i, j: (i, 0))
    attn_scratch = [pltpu.VMEM((tm, XA), F32), pltpu.VMEM((tm, XA), F32),
                    pltpu.VMEM((tm, BRANCH), BF16)]
    operands = [x, cache_k, cache_v, state] + list(weights)
    aliases = {}
    extra_specs = []
    if kind == "a":
        kern = _sample_a_kernel
        nstate = CONV_A - 1
        st_in = pl.BlockSpec((None, nseq, nstate, MIX), lambda i, j: (sidx, i, 0, 0))
        st_shape = jax.ShapeDtypeStruct((nseq_total, nstate, MIX), F32)
        st_out = pl.BlockSpec((nseq, nstate, MIX), lambda i, j: (i, 0, 0))
        scratch = [pltpu.VMEM((nseq * (HDR_A + ntok), MIX), F32)] + attn_scratch
    else:
        kern = _sample_b_kernel
        nstate = CONV_B - 1
        st_in = pl.BlockSpec((None, nstate, nseq, MIX), lambda i, j: (sidx, 0, i, 0),
                             pipeline_mode=pl.Buffered(1))
        st_shape = jax.ShapeDtypeStruct(state.shape, F32)
        st_out = pl.BlockSpec((None, nstate, nseq, MIX), lambda i, j: (sidx, 0, i, 0))
        scratch = [pltpu.VMEM((tm, MIX), F32),
                   pltpu.VMEM((tm, MIX), F32),
                   pltpu.VMEM((MIX // LANES, tm, LANES), F32)] + attn_scratch
        extra_specs = [pl.BlockSpec(memory_space=pl.ANY)]
        aliases = {len(operands): 1}
        operands = operands + [prev_state]
    in_specs = [x_spec, kv_spec, kv_spec, st_in] + wspecs + extra_specs
    return pl.pallas_call(
        kern,
        out_shape=(jax.ShapeDtypeStruct((rows, d), F32), st_shape),
        grid=(rows // tm, nj),
        in_specs=in_specs,
        out_specs=(x_spec, st_out),
        scratch_shapes=scratch,
        input_output_aliases=aliases,
        compiler_params=pltpu.CompilerParams(
            dimension_semantics=("arbitrary", "arbitrary"),
            vmem_limit_bytes=VMEM_LIMIT),
        name="sample_layer_" + kind,
    )(*operands)


def kernel(x_prompt, x_sample, mem_prompt, cache_mem_k, cache_mem_v, state_conv_a, state_conv_b, a_norm_pre, a_norm_post, a_mem_norm, a_w_in, a_conv_w, a_w_kv, a_w_out, b_norm_pre, b_norm_post, b_mem_norm, b_w_in, b_conv_w, b_conv_b, b_ln_g, b_ln_b, b_w_kv, b_w_out):
    depth = cache_mem_k.shape[0]
    n_prompt = x_prompt.shape[0]
    n_sample, ntok, d = x_sample.shape

    def rows3(v):
        return v.reshape(v.shape[0], 1, v.shape[1])

    w_a = (rows3(a_norm_pre), rows3(a_norm_post), a_w_in.astype(BF16), a_conv_w,
           a_w_out.astype(BF16))
    w_b = (rows3(b_norm_pre), rows3(b_norm_post), b_w_in.astype(BF16), b_conv_w,
           rows3(b_conv_b), rows3(b_ln_g), rows3(b_ln_b), b_w_out.astype(BF16))

    w_kv = jnp.stack([(a_w_kv if i % 2 == 0 else b_w_kv)[i // 2] for i in range(depth)]).astype(BF16)
    g_mem = jnp.stack([(a_mem_norm if i % 2 == 0 else b_mem_norm)[i // 2] for i in range(depth)])
    k_p, v_p, kb_p, vb_p = _memory_kv(mem_prompt, g_mem.reshape(depth, 1, d), w_kv)

    cache_k = cache_mem_k.reshape(depth, n_sample, N_MEM * HEADS, HEAD_DIM)
    cache_v = cache_mem_v.reshape(depth, n_sample, N_MEM * HEADS, HEAD_DIM)
    state_b = jnp.transpose(state_conv_b, (0, 2, 1, 3))
    y_p = x_prompt
    y_s = x_sample.reshape(n_sample * ntok, d)
    ca_p, cb_p, ca_s = [], [], []
    cb_s = jnp.zeros(state_b.shape, F32)
    for i in range(depth):
        j = i // 2
        if i % 2 == 0:
            y_p, st_p = _prompt_layer("a", i, j, y_p, kb_p, vb_p, w_a)
            y_s, st_s = _sample_layer("a", i, y_s, cache_k, cache_v, state_conv_a, j, w_a, ntok)
            ca_p.append(st_p)
            ca_s.append(st_s)
        else:
            y_p, st_p = _prompt_layer("b", i, j, y_p, kb_p, vb_p, w_b)
            y_s, cb_s = _sample_layer("b", i, y_s, cache_k, cache_v, state_b, j, w_b, ntok,
                                      prev_state=cb_s)
            cb_p.append(st_p)
    kv_shape = (depth, n_prompt, N_MEM, HEADS, HEAD_DIM)
    return (y_p, y_s.reshape(n_sample, ntok, d), k_p.reshape(kv_shape), v_p.reshape(kv_shape),
            jnp.stack(ca_p), jnp.stack(cb_p), jnp.stack(ca_s),
            jnp.transpose(cb_s, (0, 2, 1, 3)))
```

```python
import jax
import jax.numpy as jnp
from jax import lax
from jax.experimental import pallas as pl
from jax.experimental.pallas import tpu as pltpu

F32 = jnp.float32
BF16 = jnp.bfloat16

D_MODEL = 1024
N_MEM = 256
HEADS = 4
HEAD_DIM = 128
XA = HEADS * HEAD_DIM
BRANCH = 2 * D_MODEL
MIX = BRANCH - XA
CONV_A = 3
CONV_B = 31
EPS = 1e-6
SCALE = HEAD_DIM ** -0.5

SUBLANES = 8
LANES = 128
CHUNK = 512
PROMPT_TILE = 512
SAMPLE_TILE_SEQS_A = 2 * SUBLANES
SAMPLE_TILE_SEQS_B = SUBLANES
SAMPLE_KV_BLOCKS = 1
MEMKV_SEQS = 4
HDR_A = SUBLANES
HDR_B = 4 * SUBLANES
OFF_B = HDR_B - (CONV_B - 1)
CONV_ROWS = 64
ROW_PITCH = 2
VMEM_LIMIT = 58 * 1024 * 1024


def _dot(a, b):
    return jnp.dot(a, b, preferred_element_type=F32)


def _rms(x, g):
    ms = jnp.mean(x * x, axis=-1, keepdims=True)
    return x * lax.rsqrt(ms + EPS) * g


def _silu(z):
    return z * jax.nn.sigmoid(z)


def _layer_norm(c, g, b):
    mu = jnp.mean(c, axis=-1, keepdims=True)
    xc = c - mu
    var = jnp.mean(xc * xc, axis=-1, keepdims=True)
    return xc * lax.rsqrt(var + EPS) * g + b


def _prompt_attention(q, zx, k_ref, v_ref, br_ref):
    tm = q.shape[0]
    heads = [slice(h * HEAD_DIM, (h + 1) * HEAD_DIM) for h in range(HEADS)]
    s = jnp.concatenate(
        [lax.dot_general((q[:, hs] * SCALE).astype(BF16), k_ref[:, hs], (((1,), (1,)), ((), ())),
                         preferred_element_type=F32) for hs in heads], axis=0)
    m = jnp.max(s, axis=-1, keepdims=True)
    p = jnp.exp(s - m)
    inv_l = 1.0 / jnp.sum(p, axis=-1, keepdims=True)
    p = p.astype(BF16)
    for h, hs in enumerate(heads):
        rs = slice(h * tm, (h + 1) * tm)
        o = _dot(p[rs, :], v_ref[:, hs]) * inv_l[rs, :]
        br_ref[:, MIX + h * HEAD_DIM:MIX + (h + 1) * HEAD_DIM] = (
            o * _silu(zx[:, hs])).astype(BF16)


def _finish(x, br_ref, wout_ref, gpost_ref, y_ref):
    out = _dot(br_ref[...], wout_ref[...])
    y_ref[...] = x + _rms(out, gpost_ref[...])


def _head_rows(h):
    return pl.ds(h, N_MEM, stride=HEADS)


def _rows(start, n):
    return pl.ds(ROW_PITCH * start, n, stride=ROW_PITCH)


def _memkv_kernel(mem_ref, g_ref, w_ref, k_ref, v_ref, kb_ref, vb_ref):
    nseq = k_ref.shape[0]
    mn = _rms(mem_ref[...], g_ref[...]).astype(BF16)
    kv = _dot(mn, w_ref[...])
    for b in range(nseq):
        k = kv[b * N_MEM:(b + 1) * N_MEM, :XA]
        v = kv[b * N_MEM:(b + 1) * N_MEM, XA:]
        for h in range(HEADS):
            hs = slice(h * HEAD_DIM, (h + 1) * HEAD_DIM)
            k_ref[b, _head_rows(h), :] = k[:, hs]
            v_ref[b, _head_rows(h), :] = v[:, hs]
        kb_ref[b] = k.astype(BF16)
        vb_ref[b] = v.astype(BF16)


def _memory_kv(mem, gains, w_kv):
    nb = mem.shape[0]
    nl = w_kv.shape[0]
    ns = MEMKV_SEQS
    out_f = jax.ShapeDtypeStruct((nl, nb, N_MEM * HEADS, HEAD_DIM), F32)
    out_b = jax.ShapeDtypeStruct((nl, nb, N_MEM, XA), BF16)
    f_spec = pl.BlockSpec((None, ns, N_MEM * HEADS, HEAD_DIM), lambda l, b: (l, b, 0, 0))
    b_spec = pl.BlockSpec((None, ns, N_MEM, XA), lambda l, b: (l, b, 0, 0))
    return pl.pallas_call(
        _memkv_kernel,
        out_shape=(out_f, out_f, out_b, out_b),
        grid=(nl, nb // ns),
        in_specs=[
            pl.BlockSpec((ns * N_MEM, D_MODEL), lambda l, b: (b, 0)),
            pl.BlockSpec((None, 1, D_MODEL), lambda l, b: (l, 0, 0)),
            pl.BlockSpec((None, D_MODEL, 2 * XA), lambda l, b: (l, 0, 0)),
        ],
        out_specs=(f_spec, f_spec, b_spec, b_spec),
        compiler_params=pltpu.CompilerParams(
            dimension_semantics=("arbitrary", "arbitrary")),
        name="memory_kv",
    )(mem.reshape(nb * N_MEM, D_MODEL), gains, w_kv)


def _prompt_a_kernel(x_ref, k_ref, v_ref, gpre_ref, gpost_ref, win_ref, cw_ref, wout_ref,
                     y_ref, st_ref, cbuf, br_ref):
    tm = x_ref.shape[0]
    nsl = CHUNK // LANES

    @pl.when(pl.program_id(1) == 0)
    def _():
        for sl in range(MIX // LANES):
            cbuf[sl, _rows(0, HDR_A), :] = jnp.zeros((HDR_A, LANES), F32)

    x = x_ref[...]
    xn = _rms(x, gpre_ref[...]).astype(BF16)

    def chunk(j):
        c0 = j * CHUNK
        hin = _dot(xn, win_ref[:, c0:c0 + CHUNK])
        cg = _dot(xn, win_ref[:, 2 * MIX + c0:2 * MIX + c0 + CHUNK])
        u = cg * hin
        bg = _dot(xn, win_ref[:, MIX + c0:MIX + c0 + CHUNK])
        z = _dot(xn, win_ref[:, 3 * MIX + c0:3 * MIX + c0 + CHUNK])
        gate = bg * _silu(z)
        st_ref[:, c0:c0 + CHUNK] = u[tm - (CONV_A - 1):, :]
        for lb in range(nsl):
            sl = j * nsl + lb
            ls = slice(c0 + lb * LANES, c0 + (lb + 1) * LANES)
            ul = u[:, lb * LANES:(lb + 1) * LANES]
            cbuf[sl, _rows(HDR_A, tm), :] = ul
            conv = (cw_ref[2:3, ls] * ul
                    + cw_ref[1:2, ls] * cbuf[sl, _rows(HDR_A - 1, tm), :]
                    + cw_ref[0:1, ls] * cbuf[sl, _rows(HDR_A - 2, tm), :])
            br_ref[:, ls] = (gate[:, lb * LANES:(lb + 1) * LANES] * conv).astype(BF16)
            cbuf[sl, _rows(0, HDR_A), :] = ul[tm - HDR_A:, :]

    chunk(0)
    zx = _dot(xn, win_ref[:, 4 * MIX:4 * MIX + XA])
    q = _dot(xn, win_ref[:, 3 * MIX + BRANCH:])
    _prompt_attention(q, zx, k_ref, v_ref, br_ref)
    chunk(1)
    chunk(2)
    _finish(x, br_ref, wout_ref, gpost_ref, y_ref)


def _prompt_b_kernel(x_ref, k_ref, v_ref, gpre_ref, gpost_ref, win_ref, cw_ref, cb_ref,
                     lng_ref, lnb_ref, wout_ref, y_ref, st_ref, cbuf, cscr, br_ref):
    tm = x_ref.shape[0]
    nsl = CHUNK // LANES

    @pl.when(pl.program_id(1) == 0)
    def _():
        for sl in range(MIX // LANES):
            cbuf[sl, _rows(0, HDR_B), :] = jnp.zeros((HDR_B, LANES), F32)

    x = x_ref[...]
    xn = _rms(x, gpre_ref[...]).astype(BF16)
    for j in range(MIX // CHUNK):
        c0 = j * CHUNK
        val = _dot(xn, win_ref[:, c0:c0 + CHUNK])
        glu = _dot(xn, win_ref[:, MIX + c0:MIX + c0 + CHUNK])
        u = val * jax.nn.sigmoid(glu)
        for lb in range(nsl):
            cbuf[j * nsl + lb, _rows(HDR_B, tm), :] = u[:, lb * LANES:(lb + 1) * LANES]
    for sl in range(MIX // LANES):
        ls = slice(sl * LANES, (sl + 1) * LANES)
        for rb in range(tm // CONV_ROWS):
            r0 = rb * CONV_ROWS + OFF_B
            acc = cb_ref[:, ls] + cw_ref[0:1, ls] * cbuf[sl, _rows(r0, CONV_ROWS), :]
            for k in range(1, CONV_B):
                acc = acc + cw_ref[k:k + 1, ls] * cbuf[sl, _rows(r0 + k, CONV_ROWS), :]
            cscr[rb * CONV_ROWS:(rb + 1) * CONV_ROWS, ls] = acc
        tail = cbuf[sl, _rows(tm, HDR_B), :]
        st_ref[:, ls] = tail[OFF_B:, :]
        cbuf[sl, _rows(0, HDR_B), :] = tail
    mixv = _silu(_layer_norm(cscr[...], lng_ref[...], lnb_ref[...]))
    for j in range(MIX // CHUNK):
        c0 = j * CHUNK
        z = _dot(xn, win_ref[:, 2 * MIX + c0:2 * MIX + c0 + CHUNK])
        br_ref[:, c0:c0 + CHUNK] = (mixv[:, c0:c0 + CHUNK] * _silu(z)).astype(BF16)
    zx = _dot(xn, win_ref[:, 3 * MIX:3 * MIX + XA])
    q = _dot(xn, win_ref[:, 2 * MIX + BRANCH:])
    _prompt_attention(q, zx, k_ref, v_ref, br_ref)
    _finish(x, br_ref, wout_ref, gpost_ref, y_ref)


def _resident(w, j):
    idx = (j,) + (0,) * (w.ndim - 1)
    return pl.BlockSpec((None,) + w.shape[1:], lambda *_: idx, pipeline_mode=pl.Buffered(1))


def _prompt_layer(kind, layer, j, x, kb, vb, weights):
    nb, t, d = x.shape
    tm = PROMPT_TILE
    wspecs = [_resident(w, j) for w in weights]
    if kind == "a":
        kern = _prompt_a_kernel
        nstate = CONV_A - 1
        scratch = [pltpu.VMEM((MIX // LANES, ROW_PITCH * (HDR_A + tm), LANES), F32),
                   pltpu.VMEM((tm, BRANCH), BF16)]
    else:
        kern = _prompt_b_kernel
        nstate = CONV_B - 1
        scratch = [pltpu.VMEM((MIX // LANES, ROW_PITCH * (HDR_B + tm), LANES), F32),
                   pltpu.VMEM((tm, MIX), F32), pltpu.VMEM((tm, BRANCH), BF16)]
    kv_spec = pl.BlockSpec((None, None, N_MEM, XA), lambda b, i: (layer, b, 0, 0))
    return pl.pallas_call(
        kern,
        out_shape=(jax.ShapeDtypeStruct((nb, t, d), F32),
                   jax.ShapeDtypeStruct((nb, nstate, MIX), F32)),
        grid=(nb, t // tm),
        in_specs=[pl.BlockSpec((None, tm, d), lambda b, i: (b, i, 0)), kv_spec, kv_spec] + wspecs,
        out_specs=(pl.BlockSpec((None, tm, d), lambda b, i: (b, i, 0)),
                   pl.BlockSpec((None, nstate, MIX), lambda b, i: (b, 0, 0))),
        scratch_shapes=scratch,
        compiler_params=pltpu.CompilerParams(
            dimension_semantics=("arbitrary", "arbitrary"),
            vmem_limit_bytes=VMEM_LIMIT),
        name="prompt_layer_" + kind,
    )(x, kb, vb, *weights)


def _sample_attention(q_ref, zx_ref, k_ref, v_ref, br_ref, r0, ntok):
    nseq = k_ref.shape[0]
    rows = HEADS * ntok
    row_head = lax.broadcasted_iota(jnp.int32, (rows, XA), 0) // ntok
    lane_head = lax.broadcasted_iota(jnp.int32, (rows, XA), 1) // HEAD_DIM
    diag = row_head == lane_head
    scores = []
    for g in range(nseq):
        kg = jnp.concatenate([k_ref[g, _head_rows(h), :] for h in range(HEADS)], axis=1)
        qg = q_ref[pl.ds(r0 + g * ntok, ntok), :]
        qd = jnp.where(diag, jnp.concatenate([qg] * HEADS, axis=0), 0.0).astype(BF16)
        scores.append(lax.dot_general(qd, kg.astype(BF16), (((1,), (1,)), ((), ())),
                                      preferred_element_type=F32))
    s = jnp.concatenate(scores, axis=0)
    m = jnp.max(s, axis=-1, keepdims=True)
    p = jnp.exp(s - m)
    inv_l = 1.0 / jnp.sum(p, axis=-1, keepdims=True)
    p = p.astype(BF16)
    outs = []
    for g in range(nseq):
        vg = jnp.concatenate([v_ref[g, _head_rows(h), :] for h in range(HEADS)], axis=1)
        gs = slice(g * rows, (g + 1) * rows)
        od = _dot(p[gs, :], vg.astype(BF16)) * inv_l[gs, :]
        outs.append(jnp.concatenate(
            [od[h * ntok:(h + 1) * ntok, h * HEAD_DIM:(h + 1) * HEAD_DIM] for h in range(HEADS)],
            axis=1))
    rs = pl.ds(r0, nseq * ntok)
    br_ref[rs, MIX:] = (jnp.concatenate(outs, axis=0) * zx_ref[rs, :]).astype(BF16)


def _sample_a_kernel(x_ref, k_ref, v_ref, st_ref, gpre_ref, gpost_ref, win_ref, cw_ref, wout_ref,
                     y_ref, nst_ref, cbuf, q_ref, zx_ref, br_ref):
    nseq = st_ref.shape[0]
    ntok = x_ref.shape[0] // nseq
    nsub = k_ref.shape[0]
    per = HDR_A + ntok
    j = pl.program_id(1)

    @pl.when(j == 0)
    def _():
        xn = _rms(x_ref[...], gpre_ref[...]).astype(BF16)
        for g in range(nseq):
            cbuf[g * per + HDR_A - (CONV_A - 1):g * per + HDR_A, :] = st_ref[g]
        for c in range(MIX // CHUNK):
            c0 = c * CHUNK
            cs = slice(c0, c0 + CHUNK)
            hin = _dot(xn, win_ref[:, c0:c0 + CHUNK])
            cg = _dot(xn, win_ref[:, 2 * MIX + c0:2 * MIX + c0 + CHUNK])
            u = cg * hin
            convs = []
            for g in range(nseq):
                b0 = g * per + HDR_A
                cbuf[b0:b0 + ntok, cs] = u[g * ntok:(g + 1) * ntok, :]
                convs.append(cw_ref[2:3, cs] * cbuf[b0:b0 + ntok, cs]
                             + cw_ref[1:2, cs] * cbuf[b0 - 1:b0 - 1 + ntok, cs]
                             + cw_ref[0:1, cs] * cbuf[b0 - 2:b0 - 2 + ntok, cs])
            conv = jnp.concatenate(convs, axis=0)
            bg = _dot(xn, win_ref[:, MIX + c0:MIX + c0 + CHUNK])
            z = _dot(xn, win_ref[:, 3 * MIX + c0:3 * MIX + c0 + CHUNK])
            br_ref[:, cs] = (bg * conv * _silu(z)).astype(BF16)
        for g in range(nseq):
            b0 = g * per + HDR_A
            nst_ref[g] = cbuf[b0 + ntok - (CONV_A - 1):b0 + ntok, :]
        zx_ref[...] = _silu(_dot(xn, win_ref[:, 4 * MIX:4 * MIX + XA]))
        q_ref[...] = _dot(xn, win_ref[:, 3 * MIX + BRANCH:]) * SCALE

    r0 = pl.multiple_of(j * (nsub * ntok), nsub * ntok)
    _sample_attention(q_ref, zx_ref, k_ref, v_ref, br_ref, r0, ntok)

    @pl.when(j == pl.num_programs(1) - 1)
    def _():
        _finish(x_ref[...], br_ref, wout_ref, gpost_ref, y_ref)


def _sample_b_kernel(x_ref, k_ref, v_ref, st_ref, gpre_ref, gpost_ref, win_ref, cw_ref, cb_ref,
                     lng_ref, lnb_ref, wout_ref, *rest):
    y_ref, nst_ref, ubuf, cscr, tscr, q_ref, zx_ref, br_ref = rest[-8:]
    nstate = CONV_B - 1
    nseq = st_ref.shape[1]
    ntok = x_ref.shape[0] // nseq
    nsub = k_ref.shape[0]
    nlb = MIX // LANES
    nsl = CHUNK // LANES
    j = pl.program_id(1)

    @pl.when(j == 0)
    def _():
        xn = _rms(x_ref[...], gpre_ref[...]).astype(BF16)
        for c in range(MIX // CHUNK):
            c0 = c * CHUNK
            val = _dot(xn, win_ref[:, c0:c0 + CHUNK])
            glu = _dot(xn, win_ref[:, MIX + c0:MIX + c0 + CHUNK])
            u = val * jax.nn.sigmoid(glu)
            for lb in range(nsl):
                tscr[c * nsl + lb] = u[:, lb * LANES:(lb + 1) * LANES]
        for t in range(ntok):
            for lb in range(nlb):
                ubuf[t * nseq:(t + 1) * nseq, lb * LANES:(lb + 1) * LANES] = (
                    tscr[lb, pl.ds(t, nseq, stride=ntok), :])

        def full(i, cs):
            if i < nstate:
                return st_ref[i, :, cs]
            return ubuf[(i - nstate) * nseq:(i - nstate + 1) * nseq, cs]

        for i in range(nstate):
            nst_ref[i] = full(ntok + i, slice(None))
        for t in range(ntok):
            for c in range(MIX // CHUNK):
                cs = slice(c * CHUNK, (c + 1) * CHUNK)
                acc = cb_ref[:, cs] + cw_ref[0:1, cs] * full(t, cs)
                for k in range(1, CONV_B):
                    acc = acc + cw_ref[k:k + 1, cs] * full(t + k, cs)
                cscr[t * nseq:(t + 1) * nseq, cs] = acc
        for t in range(ntok):
            mix_t = _silu(_layer_norm(cscr[t * nseq:(t + 1) * nseq, :], lng_ref[...], lnb_ref[...]))
            for lb in range(nlb):
                tscr[lb, pl.ds(t, nseq, stride=ntok), :] = mix_t[:, lb * LANES:(lb + 1) * LANES]
        for c in range(MIX // CHUNK):
            c0 = c * CHUNK
            z = _dot(xn, win_ref[:, 2 * MIX + c0:2 * MIX + c0 + CHUNK])
            mixv = jnp.concatenate([tscr[c * nsl + lb] for lb in range(nsl)], axis=1)
            br_ref[:, c0:c0 + CHUNK] = (mixv * _silu(z)).astype(BF16)
        zx_ref[...] = _silu(_dot(xn, win_ref[:, 3 * MIX:3 * MIX + XA]))
        q_ref[...] = _dot(xn, win_ref[:, 2 * MIX + BRANCH:]) * SCALE

    r0 = pl.multiple_of(j * (nsub * ntok), nsub * ntok)
    _sample_attention(q_ref, zx_ref, k_ref, v_ref, br_ref, r0, ntok)

    @pl.when(j == pl.num_programs(1) - 1)
    def _():
        _finish(x_ref[...], br_ref, wout_ref, gpost_ref, y_ref)


def _sample_layer(kind, layer, x, cache_k, cache_v, state, sidx, weights, ntok, prev_state=None):
    rows, d = x.shape
    nseq = SAMPLE_TILE_SEQS_A if kind == "a" else SAMPLE_TILE_SEQS_B
    nj = SAMPLE_KV_BLOCKS
    nsub = nseq // nj
    tm = nseq * ntok
    nseq_total = rows // ntok
    wspecs = [_resident(w, sidx) for w in weights]
    kv_spec = pl.BlockSpec((None, nsub, N_MEM * HEADS, HEAD_DIM),
                           lambda i, j: (layer, i * nj + j, 0, 0))
    x_spec = pl.BlockSpec((tm, d), lambda i, j: (i, 0))
    attn_scratch = [pltpu.VMEM((tm, XA), F32), pltpu.VMEM((tm, XA), F32),
                    pltpu.VMEM((tm, BRANCH), BF16)]
    operands = [x, cache_k, cache_v, state] + list(weights)
    aliases = {}
    extra_specs = []
    if kind == "a":
        kern = _sample_a_kernel
        nstate = CONV_A - 1
        st_in = pl.BlockSpec((None, nseq, nstate, MIX), lambda i, j: (sidx, i, 0, 0))
        st_shape = jax.ShapeDtypeStruct((nseq_total, nstate, MIX), F32)
        st_out = pl.BlockSpec((nseq, nstate, MIX), lambda i, j: (i, 0, 0))
        scratch = [pltpu.VMEM((nseq * (HDR_A + ntok), MIX), F32)] + attn_scratch
    else:
        kern = _sample_b_kernel
        nstate = CONV_B - 1
        st_in = pl.BlockSpec((None, nstate, nseq, MIX), lambda i, j: (sidx, 0, i, 0))
        st_shape = jax.ShapeDtypeStruct(state.shape, F32)
        st_out = pl.BlockSpec((None, nstate, nseq, MIX), lambda i, j: (sidx, 0, i, 0))
        scratch = [pltpu.VMEM((tm, MIX), F32),
                   pltpu.VMEM((tm, MIX), F32),
                   pltpu.VMEM((MIX // LANES, tm, LANES), F32)] + attn_scratch
        extra_specs = [pl.BlockSpec(memory_space=pl.ANY)]
        aliases = {len(operands): 1}
        operands = operands + [prev_state]
    in_specs = [x_spec, kv_spec, kv_spec, st_in] + wspecs + extra_specs
    return pl.pallas_call(
        kern,
        out_shape=(jax.ShapeDtypeStruct((rows, d), F32), st_shape),
        grid=(rows // tm, nj),
        in_specs=in_specs,
        out_specs=(x_spec, st_out),
        scratch_shapes=scratch,
        input_output_aliases=aliases,
        compiler_params=pltpu.CompilerParams(
            dimension_semantics=("arbitrary", "arbitrary"),
            vmem_limit_bytes=VMEM_LIMIT),
        name="sample_layer_" + kind,
    )(*operands)


def kernel(x_prompt, x_sample, mem_prompt, cache_mem_k, cache_mem_v, state_conv_a, state_conv_b, a_norm_pre, a_norm_post, a_mem_norm, a_w_in, a_conv_w, a_w_kv, a_w_out, b_norm_pre, b_norm_post, b_mem_norm, b_w_in, b_conv_w, b_conv_b, b_ln_g, b_ln_b, b_w_kv, b_w_out):
    depth = cache_mem_k.shape[0]
    n_prompt = x_prompt.shape[0]
    n_sample, ntok, d = x_sample.shape

    def rows3(v):
        return v.reshape(v.shape[0], 1, v.shape[1])

    w_a = (rows3(a_norm_pre), rows3(a_norm_post), a_w_in.astype(BF16), a_conv_w,
           a_w_out.astype(BF16))
    w_b = (rows3(b_norm_pre), rows3(b_norm_post), b_w_in.astype(BF16), b_conv_w,
           rows3(b_conv_b), rows3(b_ln_g), rows3(b_ln_b), b_w_out.astype(BF16))

    w_kv = jnp.stack([(a_w_kv if i % 2 == 0 else b_w_kv)[i // 2] for i in range(depth)]).astype(BF16)
    g_mem = jnp.stack([(a_mem_norm if i % 2 == 0 else b_mem_norm)[i // 2] for i in range(depth)])
    k_p, v_p, kb_p, vb_p = _memory_kv(mem_prompt, g_mem.reshape(depth, 1, d), w_kv)

    cache_k = cache_mem_k.reshape(depth, n_sample, N_MEM * HEADS, HEAD_DIM)
    cache_v = cache_mem_v.reshape(depth, n_sample, N_MEM * HEADS, HEAD_DIM)
    state_b = jnp.transpose(state_conv_b, (0, 2, 1, 3))
    y_p = x_prompt
    y_s = x_sample.reshape(n_sample * ntok, d)
    ca_p, cb_p, ca_s = [], [], []
    cb_s = jnp.zeros(state_b.shape, F32)
    for i in range(depth):
        j = i // 2
        if i % 2 == 0:
            y_p, st_p = _prompt_layer("a", i, j, y_p, kb_p, vb_p, w_a)
            y_s, st_s = _sample_layer("a", i, y_s, cache_k, cache_v, state_conv_a, j, w_a, ntok)
            ca_p.append(st_p)
            ca_s.append(st_s)
        else:
            y_p, st_p = _prompt_layer("b", i, j, y_p, kb_p, vb_p, w_b)
            y_s, cb_s = _sample_layer("b", i, y_s, cache_k, cache_v, state_b, j, w_b, ntok,
                                      prev_state=cb_s)
            cb_p.append(st_p)
    kv_shape = (depth, n_prompt, N_MEM, HEADS, HEAD_DIM)
    return (y_p, y_s.reshape(n_sample, ntok, d), k_p.reshape(kv_shape), v_p.reshape(kv_shape),
            jnp.stack(ca_p), jnp.stack(cb_p), jnp.stack(ca_s),
            jnp.transpose(cb_s, (0, 2, 1, 3)))
```
